```python
import math
import jax, jax.numpy as jnp
from jax import lax
import numpy as np

D_MODEL = 4096
BATCH = 1
SEQ = 16384
DEPTH = 4
DEC_BATCH = 8
DEC_SEQ = 2048
PAST_LEN = 128

HEAD_DIM = 128
N_META = 16
GRID_W = 64
EPS = 1e-6
A_HEADS = 4
A_KV_HEADS = 2
A_GROUP = A_HEADS // A_KV_HEADS
Q_BLOCK = 128
ROPE_THETA = 10000.0
B_HEADS = 8
CONV_K = 5
CHUNK = 64
C_HEADS = 4
WIN_ROWS = 8
WIN_COLS = 16
A_Q_W = A_HEADS * HEAD_DIM
A_KV_W = A_KV_HEADS * HEAD_DIM
B_W = B_HEADS * HEAD_DIM
C_W = C_HEADS * HEAD_DIM
MIX_WIDTH = A_Q_W + B_W + C_W
IN_WIDTH = A_Q_W + 2 * A_KV_W + 4 * B_W + 4 * B_HEADS + 3 * C_W
D_FF = 2048

kernel_name = "hybrid_parallel_group_bidir_encoder"


def rms_norm(x, gain):
    xf = x.astype(jnp.float32)
    y = xf * lax.rsqrt(jnp.mean(xf * xf, axis=-1, keepdims=True) + EPS)
    return (y * gain.astype(jnp.float32)).astype(x.dtype)


def swiglu(x, w_gate, w_up, w_down):
    return (jax.nn.silu(x @ w_gate) * (x @ w_up)) @ w_down


def l2norm(x):
    return x * lax.rsqrt(jnp.sum(x * x, axis=-1, keepdims=True) + EPS)


def in_split_points():
    sizes = (A_Q_W, A_KV_W, A_KV_W, 3 * B_W, B_W, 2 * B_HEADS, 2 * B_HEADS, C_W, C_W, C_W)
    return [int(s) for s in np.cumsum(sizes)[:-1]]


def grid_positions(n_tok):
    t = jnp.arange(n_tok, dtype=jnp.int32)
    zeros = jnp.zeros((N_META,), jnp.int32)
    return jnp.concatenate([zeros, t // GRID_W]), jnp.concatenate([zeros, t % GRID_W])


def rope_axis(x, pos):
    half = x.shape[-1] // 2
    freqs = ROPE_THETA ** (-jnp.arange(half, dtype=jnp.float32) / half)
    ang = pos.astype(jnp.float32)[:, None] * freqs[None, :]
    cos = jnp.cos(ang)[None, :, None, :]
    sin = jnp.sin(ang)[None, :, None, :]
    x1, x2 = x[..., :half], x[..., half:]
    return jnp.concatenate([x1 * cos - x2 * sin, x2 * cos + x1 * sin], axis=-1)


def axial_rope(x, row, col):
    h = x.shape[-1] // 2
    xf = x.astype(jnp.float32)
    return jnp.concatenate([rope_axis(xf[..., :h], row), rope_axis(xf[..., h:], col)], axis=-1).astype(x.dtype)


def mixer_attention(q, k, v, q_gain, k_gain, row, col):
    bsz, L = q.shape[:2]
    dt = q.dtype
    q = axial_rope(rms_norm(q, q_gain), row, col) * (HEAD_DIM ** -0.5)
    k = axial_rope(rms_norm(k, k_gain), row, col)
    n_blk = -(-L // Q_BLOCK)
    pad = n_blk * Q_BLOCK - L
    qb = jnp.pad(q, ((0, 0), (0, pad), (0, 0), (0, 0)))
    qb = qb.reshape(bsz, n_blk, Q_BLOCK, A_KV_HEADS, A_GROUP, HEAD_DIM)
    qb = jnp.moveaxis(qb, 1, 0)

    def block(qi):
        s = jnp.einsum('bqkgd,bskd->bkgqs', qi, k, preferred_element_type=jnp.float32)
        e = jnp.exp(s - jnp.max(s, axis=-1, keepdims=True))
        denom = jnp.sum(e, axis=-1)
        o = jnp.einsum('bkgqs,bskd->bkgqd', e.astype(dt), v, preferred_element_type=jnp.float32)
        o = (o / denom[..., None]).astype(dt)
        return jnp.einsum('bkgqd->bqkgd', o)

    o = lax.map(block, qb)
    o = jnp.moveaxis(o, 0, 1).reshape(bsz, n_blk * Q_BLOCK, A_Q_W)
    return o[:, :L]


def short_conv(x, w):
    ch = x.shape[-1]
    y = lax.conv_general_dilated(x, w[:, None, :].astype(x.dtype), window_strides=(1,),
                                 padding=[(CONV_K // 2, CONV_K // 2)],
                                 dimension_numbers=('NWC', 'WIO', 'NWC'),
                                 feature_group_count=ch)
    return jax.nn.silu(y)


def gated_delta_chunked(q, k, v, g, beta):
    bsz, T, H, Dk = q.shape
    Dv = v.shape[-1]
    nc = T // CHUNK

    def chunks(t):
        t = t.reshape((bsz, nc, CHUNK, H) + t.shape[3:])
        return jnp.moveaxis(t, 3, 1)

    q, k, v, g, beta = (chunks(t) for t in (q, k, v, g, beta))
    gc = jnp.cumsum(g, axis=-1)
    tri = jnp.tril(jnp.ones((CHUNK, CHUNK), bool))
    strict = jnp.tril(jnp.ones((CHUNK, CHUNK), bool), -1)
    decay = jnp.exp(jnp.where(tri, gc[..., :, None] - gc[..., None, :], -jnp.inf))
    kb = k * beta[..., None]
    lmat = jnp.where(strict, jnp.einsum('bhnid,bhnjd->bhnij', kb, k) * decay, 0.0)
    a_mat = lmat + jnp.eye(CHUNK, dtype=lmat.dtype)
    w = lax.linalg.triangular_solve(a_mat, kb * jnp.exp(gc)[..., None], left_side=True, lower=True, unit_diagonal=True)
    u = lax.linalg.triangular_solve(a_mat, v * beta[..., None], left_side=True, lower=True, unit_diagonal=True)
    qk = jnp.einsum('bhnid,bhnjd->bhnij', q, k) * decay
    q_dec = q * jnp.exp(gc)[..., None]
    g_last = gc[..., -1]
    k_dec = k * jnp.exp(g_last[..., None] - gc)[..., None]

    def step(S, xs):
        w_i, u_i, qk_i, qd_i, kd_i, gl_i = xs
        v_new = u_i - jnp.einsum('bhcd,bhde->bhce', w_i, S)
        o_i = jnp.einsum('bhcd,bhde->bhce', qd_i, S) + jnp.einsum('bhij,bhje->bhie', qk_i, v_new)
        S = S * jnp.exp(gl_i)[..., None, None] + jnp.einsum('bhcd,bhce->bhde', kd_i, v_new)
        return S, o_i

    xs = tuple(jnp.moveaxis(t, 2, 0) for t in (w, u, qk, q_dec, k_dec, g_last))
    S0 = jnp.zeros((bsz, H, Dk, Dv), jnp.float32)
    _, o = lax.scan(step, S0, xs)
    o = jnp.moveaxis(o, 0, 2)
    return jnp.moveaxis(o, 1, 3).reshape(bsz, T, H, Dv)


def mixer_delta(qkv, z, a, b, conv_w, a_log, dt_bias, out_gain):
    bsz, L = qkv.shape[:2]
    dt = qkv.dtype
    qkv = short_conv(qkv, conv_w).astype(jnp.float32)
    q, k, v = jnp.split(qkv, 3, axis=-1)
    q = l2norm(q.reshape(bsz, L, B_HEADS, HEAD_DIM)) * (HEAD_DIM ** -0.5)
    k = l2norm(k.reshape(bsz, L, B_HEADS, HEAD_DIM))
    v = v.reshape(bsz, L, B_HEADS, HEAD_DIM)
    a = a.astype(jnp.float32).reshape(bsz, L, 2, B_HEADS)
    b = b.astype(jnp.float32).reshape(bsz, L, 2, B_HEADS)
    g = -jnp.exp(a_log.astype(jnp.float32)) * jax.nn.softplus(a + dt_bias.astype(jnp.float32))
    beta = jax.nn.sigmoid(b)
    pad = CHUNK - N_META

    def padt(t):
        return jnp.pad(t, ((0, 0), (pad, 0)) + ((0, 0),) * (t.ndim - 2))

    q, k, v, g, beta = (padt(t) for t in (q, k, v, g, beta))
    o_fw = gated_delta_chunked(q, k, v, g[:, :, 0], beta[:, :, 0])

    def flip(t):
        return jnp.flip(t, axis=1)

    o_bw = flip(gated_delta_chunked(flip(q), flip(k), flip(v), flip(g[:, :, 1]), flip(beta[:, :, 1])))
    o = (o_fw + o_bw)[:, pad:]
    zf = z.astype(jnp.float32).reshape(bsz, L, B_HEADS, HEAD_DIM)
    o = rms_norm(o, out_gain) * jax.nn.silu(zf)
    return o.reshape(bsz, L, B_W).astype(dt)


def mixer_neighbourhood(q, k, v, rel_bias, meta_bias):
    bsz, L = q.shape[:2]
    dt = q.dtype
    n_tok = L - N_META
    rows = n_tok // GRID_W
    wr = min(WIN_ROWS, rows)
    q = q * (HEAD_DIM ** -0.5)
    qm, km, vm = q[:, :N_META], k[:, :N_META], v[:, :N_META]
    kg = k[:, N_META:].reshape(bsz, rows, GRID_W, C_HEADS, HEAD_DIM)
    vg = v[:, N_META:].reshape(bsz, rows, GRID_W, C_HEADS, HEAD_DIM)
    row_off = jnp.arange(wr, dtype=jnp.int32)
    band_col = jnp.arange(GRID_W, dtype=jnp.int32)
    rel_bias = rel_bias.astype(jnp.float32)
    meta_bias = meta_bias.astype(jnp.float32)

    def attend(qb, q_row, q_col, row_start, col_start):
        nq = qb.shape[1]
        k_rows = lax.dynamic_slice_in_dim(kg, row_start, wr, axis=1).reshape(bsz, wr * GRID_W, C_HEADS, HEAD_DIM)
        v_rows = lax.dynamic_slice_in_dim(vg, row_start, wr, axis=1).reshape(bsz, wr * GRID_W, C_HEADS, HEAD_DIM)
        s_band = jnp.einsum('bqhd,bshd->bhqs', qb, k_rows, preferred_element_type=jnp.float32)
        s_band = s_band.reshape(bsz, C_HEADS, nq, wr, GRID_W)
        dr = row_start + row_off[None, :] - q_row[:, None] + (WIN_ROWS - 1)
        dc = jnp.clip(band_col[None, :] - q_col[:, None] + (WIN_COLS - 1), 0, 2 * WIN_COLS - 2)
        in_win = (band_col[None, :] >= col_start[:, None]) & (band_col[None, :] < col_start[:, None] + WIN_COLS)
        bias = rel_bias[:, dr[:, :, None], dc[:, None, :]]
        s_band = jnp.where(in_win[None, None, :, None, :], s_band + bias[None], -jnp.inf)
        s_meta = jnp.einsum('bqhd,bmhd->bhqm', qb, km, preferred_element_type=jnp.float32) + meta_bias[None, :, None, :]
        s = jnp.concatenate([s_meta, s_band.reshape(bsz, C_HEADS, nq, wr * GRID_W)], axis=-1)
        p = jax.nn.softmax(s, axis=-1).astype(dt)
        return (jnp.einsum('bhqm,bmhd->bqhd', p[..., :N_META], vm)
                + jnp.einsum('bhqs,bshd->bqhd', p[..., N_META:], v_rows))

    grid_col_start = jnp.clip(band_col - WIN_COLS // 2, 0, GRID_W - WIN_COLS)
    qg = jnp.moveaxis(q[:, N_META:].reshape(bsz, rows, GRID_W, C_HEADS, HEAD_DIM), 1, 0)

    def row_block(args):
        i, qb = args
        rs = jnp.clip(i - wr // 2, 0, rows - wr)
        return attend(qb, jnp.full((GRID_W,), i, jnp.int32), band_col, rs, grid_col_start)

    og = lax.map(row_block, (jnp.arange(rows, dtype=jnp.int32), qg))
    og = jnp.moveaxis(og, 0, 1).reshape(bsz, n_tok, C_W)
    zeros = jnp.zeros((N_META,), jnp.int32)
    om = attend(qm, zeros, zeros, 0, zeros).reshape(bsz, N_META, C_W)
    return jnp.concatenate([om, og], axis=1)


def encoder_trunk(x, meta_tokens, ffn1_norm, ffn1_w_gate, ffn1_w_up, ffn1_w_down, mix_norm, w_in,
                  attn_q_norm, attn_k_norm, dn_conv_w, dn_a_log, dn_dt_bias, dn_out_norm,
                  na_rel_bias, na_meta_bias, w_out, ffn2_norm, ffn2_w_gate, ffn2_w_up, ffn2_w_down, final_norm):
    bsz, n_tok, _ = x.shape
    L = N_META + n_tok
    row, col = grid_positions(n_tok)
    meta = jnp.broadcast_to(meta_tokens.astype(x.dtype)[None], (bsz, N_META, D_MODEL))
    x = jnp.concatenate([meta, x], axis=1)
    split = in_split_points()
    for l in range(DEPTH):
        h = rms_norm(x, ffn1_norm[l])
        x = x + 0.5 * swiglu(h, ffn1_w_gate[l], ffn1_w_up[l], ffn1_w_down[l])
        h = rms_norm(x, mix_norm[l])
        aq, ak, av, bqkv, bz, ba, bb, cq, ck, cv = jnp.split(h @ w_in[l], split, axis=-1)
        o_a = mixer_attention(aq.reshape(bsz, L, A_HEADS, HEAD_DIM), ak.reshape(bsz, L, A_KV_HEADS, HEAD_DIM),
                              av.reshape(bsz, L, A_KV_HEADS, HEAD_DIM), attn_q_norm[l], attn_k_norm[l], row, col)
        o_b = mixer_delta(bqkv, bz, ba, bb, dn_conv_w[l], dn_a_log[l], dn_dt_bias[l], dn_out_norm[l])
        o_c = mixer_neighbourhood(cq.reshape(bsz, L, C_HEADS, HEAD_DIM), ck.reshape(bsz, L, C_HEADS, HEAD_DIM),
                                  cv.reshape(bsz, L, C_HEADS, HEAD_DIM), na_rel_bias[l], na_meta_bias[l])
        x = x + jnp.concatenate([o_a, o_b, o_c], axis=-1) @ w_out[l]
        h = rms_norm(x, ffn2_norm[l])
        x = x + 0.5 * swiglu(h, ffn2_w_gate[l], ffn2_w_up[l], ffn2_w_down[l])
    return rms_norm(x, final_norm)[:, N_META:]


def setup_inputs(seed: int = 0) -> dict:
    key = jax.random.key(seed)
    ks = jax.random.split(key, 24)
    f32 = jnp.float32

    def nrm(k, shape, scale):
        return jax.random.normal(k, shape, f32) * scale

    def gain(k, shape):
        return 1.0 + 0.02 * jax.random.normal(k, shape, f32)

    dt_init = jnp.exp(jax.random.uniform(ks[13], (DEPTH, 2, B_HEADS), f32, math.log(1e-3), math.log(1e-1)))
    return {
        "x_prompt": jax.random.normal(ks[0], (BATCH, SEQ, D_MODEL), f32),
        "x_sample": jax.random.normal(ks[1], (DEC_BATCH, DEC_SEQ, D_MODEL), f32),
        "meta_tokens": nrm(ks[2], (N_META, D_MODEL), 1.0),
        "ffn1_norm": gain(ks[3], (DEPTH, D_MODEL)),
        "ffn1_w_gate": nrm(ks[4], (DEPTH, D_MODEL, D_FF), D_MODEL ** -0.5),
        "ffn1_w_up": nrm(ks[5], (DEPTH, D_MODEL, D_FF), D_MODEL ** -0.5),
        "ffn1_w_down": nrm(ks[6], (DEPTH, D_FF, D_MODEL), D_FF ** -0.5),
        "mix_norm": gain(ks[7], (DEPTH, D_MODEL)),
        "w_in": nrm(ks[8], (DEPTH, D_MODEL, IN_WIDTH), D_MODEL ** -0.5),
        "attn_q_norm": gain(ks[9], (DEPTH, HEAD_DIM)),
        "attn_k_norm": gain(ks[10], (DEPTH, HEAD_DIM)),
        "dn_conv_w": nrm(ks[11], (DEPTH, CONV_K, 3 * B_W), CONV_K ** -0.5),
        "dn_a_log": jnp.log(jax.random.uniform(ks[12], (DEPTH, 2, B_HEADS), f32, 1.0, 16.0)),
        "dn_dt_bias": jnp.log(jnp.expm1(dt_init)),
        "dn_out_norm": gain(ks[14], (DEPTH, HEAD_DIM)),
        "na_rel_bias": nrm(ks[15], (DEPTH, C_HEADS, 2 * WIN_ROWS - 1, 2 * WIN_COLS - 1), 0.1),
        "na_meta_bias": nrm(ks[16], (DEPTH, C_HEADS, N_META), 0.1),
        "w_out": nrm(ks[17], (DEPTH, MIX_WIDTH, D_MODEL), MIX_WIDTH ** -0.5),
        "ffn2_norm": gain(ks[18], (DEPTH, D_MODEL)),
        "ffn2_w_gate": nrm(ks[19], (DEPTH, D_MODEL, D_FF), D_MODEL ** -0.5),
        "ffn2_w_up": nrm(ks[20], (DEPTH, D_MODEL, D_FF), D_MODEL ** -0.5),
        "ffn2_w_down": nrm(ks[21], (DEPTH, D_FF, D_MODEL), D_FF ** -0.5),
        "final_norm": gain(ks[22], (D_MODEL,)),
    }


def reference(x_prompt, x_sample, meta_tokens, ffn1_norm, ffn1_w_gate, ffn1_w_up, ffn1_w_down, mix_norm, w_in,
              attn_q_norm, attn_k_norm, dn_conv_w, dn_a_log, dn_dt_bias, dn_out_norm, na_rel_bias, na_meta_bias,
              w_out, ffn2_norm, ffn2_w_gate, ffn2_w_up, ffn2_w_down, final_norm):
    y_prompt = encoder_trunk(x_prompt, meta_tokens, ffn1_norm, ffn1_w_gate, ffn1_w_up, ffn1_w_down, mix_norm, w_in,
                             attn_q_norm, attn_k_norm, dn_conv_w, dn_a_log, dn_dt_bias, dn_out_norm,
                             na_rel_bias, na_meta_bias, w_out, ffn2_norm, ffn2_w_gate, ffn2_w_up, ffn2_w_down, final_norm)
    y_sample = encoder_trunk(x_sample, meta_tokens, ffn1_norm, ffn1_w_gate, ffn1_w_up, ffn1_w_down, mix_norm, w_in,
                             attn_q_norm, attn_k_norm, dn_conv_w, dn_a_log, dn_dt_bias, dn_out_norm,
                             na_rel_bias, na_meta_bias, w_out, ffn2_norm, ffn2_w_gate, ffn2_w_up, ffn2_w_down, final_norm)
    return (y_prompt, y_sample)
```

```python
import functools

import jax
import jax.numpy as jnp
from jax import lax
from jax.experimental import pallas as pl
from jax.experimental.pallas import tpu as pltpu

F32 = jnp.float32
BF16 = jnp.bfloat16

HEAD_DIM = 128
N_META = 16
GRID_W = 64
EPS = 1e-6
A_HEADS = 4
A_KV_HEADS = 2
ROPE_THETA = 10000.0
B_HEADS = 8
CONV_K = 5
CHUNK = 64
C_HEADS = 4
WIN_ROWS = 8
WIN_COLS = 16

A_Q_W = A_HEADS * HEAD_DIM
A_KV_W = A_KV_HEADS * HEAD_DIM
B_W = B_HEADS * HEAD_DIM
C_W = C_HEADS * HEAD_DIM
MIX_WIDTH = A_Q_W + B_W + C_W

BLK = 64
HEAD_PAD = BLK - N_META
SEQ_ALIGN = 128
NEG = -1e30

COL_AQ = 0
COL_AK = COL_AQ + A_Q_W
COL_AV = COL_AK + A_KV_W
COL_BQKV = COL_AV + A_KV_W
COL_BZ = COL_BQKV + 3 * B_W
COL_CQ = COL_BZ + B_W
COL_CK = COL_CQ + C_W
COL_CV = COL_CK + C_W
COL_GATE = COL_CV + C_W
IN_TILE = 768
IN_WIDTH_PAD = -(-(COL_GATE + 128) // IN_TILE) * IN_TILE

V7X_VMEM_BYTES = 64 * 1024 * 1024
VMEM_LIMIT = 52 * 1024 * 1024


def _params(sem):
    return pltpu.CompilerParams(dimension_semantics=sem, vmem_limit_bytes=VMEM_LIMIT)


def _pick_tile(n, target, mult):
    best = None
    for t in range(mult, min(n, target) + 1, mult):
        if n % t == 0:
            best = t
    assert best is not None, (n, target, mult)
    return best


def _rmsnorm_kernel(x_ref, g_ref, o_ref):
    x = x_ref[...]
    ms = jnp.mean(x * x, axis=-1, keepdims=True)
    o_ref[...] = (x * lax.rsqrt(ms + EPS) * g_ref[...]).astype(o_ref.dtype)


def rmsnorm(x, gain, out_dtype):
    t, d = x.shape
    tm = _pick_tile(t, 320, 16)
    return pl.pallas_call(
        _rmsnorm_kernel,
        grid=(t // tm,),
        in_specs=[pl.BlockSpec((tm, d), lambda i: (i, 0)), pl.BlockSpec((1, d), lambda i: (0, 0))],
        out_specs=pl.BlockSpec((tm, d), lambda i: (i, 0)),
        out_shape=jax.ShapeDtypeStruct((t, d), out_dtype),
        compiler_params=_params(("parallel",)),
        name="rmsnorm",
    )(x, gain.reshape(1, d))


def _gate_up_kernel(h_ref, wg_ref, wu_ref, o_ref):
    h = h_ref[...]
    g = jnp.dot(h, wg_ref[...], preferred_element_type=F32)
    u = jnp.dot(h, wu_ref[...], preferred_element_type=F32)
    o_ref[...] = (g * jax.nn.sigmoid(g) * u).astype(o_ref.dtype)


def gate_up(h, wg, wu, layer):
    t, d = h.shape
    f = wg.shape[-1]
    tm = _pick_tile(t, 640, 16)
    tn = _pick_tile(f, 512, 128)
    return pl.pallas_call(
        _gate_up_kernel,
        grid=(f // tn, t // tm),
        in_specs=[
            pl.BlockSpec((tm, d), lambda j, i: (i, 0)),
            pl.BlockSpec((None, d, tn), lambda j, i: (layer, 0, j)),
            pl.BlockSpec((None, d, tn), lambda j, i: (layer, 0, j)),
        ],
        out_specs=pl.BlockSpec((tm, tn), lambda j, i: (i, j)),
        out_shape=jax.ShapeDtypeStruct((t, f), BF16),
        compiler_params=_params(("parallel", "parallel")),
        name="gate_up",
    )(h, wg, wu)


def _matmul_kernel(a_ref, w_ref, o_ref):
    o_ref[...] = jnp.dot(a_ref[...], w_ref[...], preferred_element_type=F32).astype(o_ref.dtype)


def matmul(a, w, layer, tn_target, out_dtype):
    t, k = a.shape
    n = w.shape[-1]
    tm = _pick_tile(t, 640, 16)
    tn = _pick_tile(n, tn_target, 128)
    return pl.pallas_call(
        _matmul_kernel,
        grid=(n // tn, t // tm),
        in_specs=[
            pl.BlockSpec((tm, k), lambda j, i: (i, 0)),
            pl.BlockSpec((None, k, tn), lambda j, i: (layer, 0, j)),
        ],
        out_specs=pl.BlockSpec((tm, tn), lambda j, i: (i, j)),
        out_shape=jax.ShapeDtypeStruct((t, n), out_dtype),
        compiler_params=_params(("parallel", "parallel")),
        name="matmul",
    )(a, w)


def _residual_matmul_kernel(x_ref, a_ref, w_ref, o_ref, *, scale):
    o_ref[...] = x_ref[...] + scale * jnp.dot(a_ref[...], w_ref[...], preferred_element_type=F32)


def residual_matmul(x, a, w, layer, scale):
    t, d = x.shape
    k = a.shape[-1]
    tm = _pick_tile(t, 640, 16)
    tn = _pick_tile(d, 1024, 128)
    return pl.pallas_call(
        functools.partial(_residual_matmul_kernel, scale=scale),
        grid=(d // tn, t // tm),
        in_specs=[
            pl.BlockSpec((tm, tn), lambda j, i: (i, j)),
            pl.BlockSpec((tm, k), lambda j, i: (i, 0)),
            pl.BlockSpec((None, k, tn), lambda j, i: (layer, 0, j)),
        ],
        out_specs=pl.BlockSpec((tm, tn), lambda j, i: (i, j)),
        out_shape=jax.ShapeDtypeStruct((t, d), F32),
        compiler_params=_params(("parallel", "parallel")),
        name="residual_matmul",
    )(x, a, w)


def final_rmsnorm(x, gain, grp):
    d = x.shape[-1]
    nblk = grp.lp // BLK
    base = grp.base // BLK
    rows = grp.n_tok // BLK
    return pl.pallas_call(
        _rmsnorm_kernel,
        grid=(grp.bsz, rows),
        in_specs=[
            pl.BlockSpec((BLK, d), lambda b, i: (base + b * nblk + 1 + i, 0)),
            pl.BlockSpec((1, d), lambda b, i: (0, 0)),
        ],
        out_specs=pl.BlockSpec((None, BLK, d), lambda b, i: (b, i, 0)),
        out_shape=jax.ShapeDtypeStruct((grp.bsz, grp.n_tok, d), F32),
        compiler_params=_params(("parallel", "parallel")),
        name="final_norm",
    )(x, gain.reshape(1, d))


class Group:
    def __init__(self, bsz, n_tok, base):
        assert n_tok % GRID_W == 0 and n_tok // GRID_W >= WIN_ROWS
        self.bsz = bsz
        self.n_tok = n_tok
        self.lp = -(-(BLK + n_tok) // SEQ_ALIGN) * SEQ_ALIGN
        self.base = base
        self.rows = n_tok // GRID_W
        self.valid_end = BLK + n_tok

    @property
    def total(self):
        return self.bsz * self.lp


def _position_tables(grp):
    p = jnp.arange(grp.lp, dtype=jnp.int32)
    t = jnp.clip(p - BLK, 0, grp.n_tok - 1)
    is_tok = (p >= BLK) & (p < grp.valid_end)
    row = jnp.where(is_tok, t // GRID_W, 0).astype(F32)
    col = jnp.where(is_tok, t % GRID_W, 0).astype(F32)
    half = HEAD_DIM // 4
    freqs = ROPE_THETA ** (-jnp.arange(half, dtype=F32) / half)
    ang_r = row[:, None] * freqs[None, :]
    ang_c = col[:, None] * freqs[None, :]
    cos = jnp.concatenate([jnp.cos(ang_r), jnp.cos(ang_r), jnp.cos(ang_c), jnp.cos(ang_c)], axis=-1)
    sin = jnp.concatenate([-jnp.sin(ang_r), jnp.sin(ang_r), -jnp.sin(ang_c), jnp.sin(ang_c)], axis=-1)
    valid = (p >= HEAD_PAD) & (p < grp.valid_end)
    key_bias = jnp.where(valid, 0.0, NEG).astype(F32)
    return cos, sin, key_bias


def _swap_halves(x):
    lane = lax.broadcasted_iota(jnp.int32, x.shape, 1)
    return jnp.where(lane % 64 < 32, pltpu.roll(x, 96, 1), pltpu.roll(x, 32, 1))


def _prep_ac_kernel(a_ref, cq_ref, ck_ref, cv_ref, cos_ref, sin_ref, qg_ref, kg_ref,
                    qa_ref, ka_ref, va_ref, qc_ref, kvc_ref):
    cos = cos_ref[...]
    sin = sin_ref[...]
    scale = HEAD_DIM ** -0.5

    def norm_rope(x, gain):
        ms = jnp.mean(x * x, axis=-1, keepdims=True)
        y = x * lax.rsqrt(ms + EPS) * gain
        return y * cos + _swap_halves(y) * sin

    for h in range(A_HEADS):
        x = a_ref[:, COL_AQ + h * HEAD_DIM:COL_AQ + (h + 1) * HEAD_DIM]
        qa_ref[:, h * HEAD_DIM:(h + 1) * HEAD_DIM] = (norm_rope(x, qg_ref[...]) * scale).astype(BF16)
    for h in range(A_KV_HEADS):
        x = a_ref[:, COL_AK + h * HEAD_DIM:COL_AK + (h + 1) * HEAD_DIM]
        ka_ref[:, h * HEAD_DIM:(h + 1) * HEAD_DIM] = norm_rope(x, kg_ref[...]).astype(BF16)
    va_ref[...] = a_ref[:, COL_AV:COL_AV + A_KV_W].astype(BF16)
    qc_ref[...] = (cq_ref[...] * scale).astype(BF16)
    kvc_ref[:, :C_W] = ck_ref[...].astype(BF16)
    kvc_ref[:, C_W:] = cv_ref[...].astype(BF16)


def prep_ac(proj, grp, cos, sin, q_gain, k_gain):
    tr = 128
    nb = grp.lp // tr
    base = grp.base // tr
    a_w = A_Q_W + 2 * A_KV_W

    def rowmap(col):
        return lambda b, i: (base + b * nb + i, col)

    out_map = lambda b, i: (b * nb + i, 0)
    n = grp.total
    return pl.pallas_call(
        _prep_ac_kernel,
        grid=(grp.bsz, nb),
        in_specs=[
            pl.BlockSpec((tr, a_w), rowmap(0)),
            pl.BlockSpec((tr, C_W), rowmap(COL_CQ // C_W)),
            pl.BlockSpec((tr, C_W), rowmap(COL_CK // C_W)),
            pl.BlockSpec((tr, C_W), rowmap(COL_CV // C_W)),
            pl.BlockSpec((tr, HEAD_DIM), lambda b, i: (i, 0)),
            pl.BlockSpec((tr, HEAD_DIM), lambda b, i: (i, 0)),
            pl.BlockSpec((1, HEAD_DIM), lambda b, i: (0, 0)),
            pl.BlockSpec((1, HEAD_DIM), lambda b, i: (0, 0)),
        ],
        out_specs=[
            pl.BlockSpec((tr, A_Q_W), out_map),
            pl.BlockSpec((tr, A_KV_W), out_map),
            pl.BlockSpec((tr, A_KV_W), out_map),
            pl.BlockSpec((tr, C_W), out_map),
            pl.BlockSpec((tr, 2 * C_W), out_map),
        ],
        out_shape=[
            jax.ShapeDtypeStruct((n, A_Q_W), BF16),
            jax.ShapeDtypeStruct((n, A_KV_W), BF16),
            jax.ShapeDtypeStruct((n, A_KV_W), BF16),
            jax.ShapeDtypeStruct((n, C_W), BF16),
            jax.ShapeDtypeStruct((n, 2 * C_W), BF16),
        ],
        compiler_params=_params(("parallel", "parallel")),
        name="prep_ac",
    )(proj, proj, proj, proj, cos, sin, q_gain.reshape(1, HEAD_DIM), k_gain.reshape(1, HEAD_DIM))


def _flash_kernel(q_ref, k_ref, v_ref, bias_ref, o_ref, *, tq, ck, nchunks):
    q2 = jnp.concatenate([q_ref[:, :HEAD_DIM], q_ref[:, HEAD_DIM:]], axis=0)

    def step(c, carry):
        m, l, acc = carry
        start = pl.multiple_of(c * ck, ck)
        k = k_ref[pl.ds(start, ck), :]
        v = v_ref[pl.ds(start, ck), :]
        s = lax.dot_general(q2, k, (((1,), (1,)), ((), ())), preferred_element_type=F32)
        s = s + bias_ref[pl.ds(c, 1), :]
        m_new = jnp.maximum(m, jnp.max(s, axis=-1, keepdims=True))
        p = jnp.exp(s - m_new)
        alpha = jnp.exp(m - m_new)
        l = alpha * l + jnp.sum(p, axis=-1, keepdims=True)
        acc = alpha * acc + jnp.dot(p.astype(BF16), v, preferred_element_type=F32)
        return m_new, l, acc

    init = (jnp.full((2 * tq, 1), -jnp.inf, F32), jnp.zeros((2 * tq, 1), F32), jnp.zeros((2 * tq, HEAD_DIM), F32))
    if nchunks == 1:
        _, l, acc = step(0, init)
    else:
        _, l, acc = lax.fori_loop(0, nchunks, step, init)
    o = acc / l
    o_ref[:, :HEAD_DIM] = o[:tq].astype(o_ref.dtype)
    o_ref[:, HEAD_DIM:] = o[tq:].astype(o_ref.dtype)


def flash_attention(qa, ka, va, key_bias, grp):
    lp = grp.lp
    tq = _pick_tile(lp, 384, 128)
    ck = _pick_tile(lp, 512, 128) if lp > 4096 else lp
    nchunks = lp // ck
    nq = lp // tq
    gw = 2 * HEAD_DIM
    return pl.pallas_call(
        functools.partial(_flash_kernel, tq=tq, ck=ck, nchunks=nchunks),
        grid=(grp.bsz, A_KV_HEADS, nq),
        in_specs=[
            pl.BlockSpec((tq, gw), lambda b, g, i: (b * nq + i, g)),
            pl.BlockSpec((lp, HEAD_DIM), lambda b, g, i: (b, g)),
            pl.BlockSpec((lp, HEAD_DIM), lambda b, g, i: (b, g)),
            pl.BlockSpec((nchunks, ck), lambda b, g, i: (0, 0)),
        ],
        out_specs=pl.BlockSpec((tq, gw), lambda b, g, i: (b * nq + i, g)),
        out_shape=jax.ShapeDtypeStruct((grp.total, A_Q_W), BF16),
        compiler_params=_params(("parallel", "parallel", "parallel")),
        name="flash_attention",
    )(qa, ka, va, key_bias.reshape(nchunks, ck))


NA_KEYS = BLK + WIN_ROWS * GRID_W
NA_META_VARIANT = WIN_ROWS


def _na_bias_table(rel_bias, meta_bias):
    rel_bias = rel_bias.astype(F32)
    qc = jnp.arange(GRID_W, dtype=jnp.int32)
    c = jnp.arange(GRID_W, dtype=jnp.int32)
    col_start = jnp.clip(qc - WIN_COLS // 2, 0, GRID_W - WIN_COLS)
    in_win = (c[None, :] >= col_start[:, None]) & (c[None, :] < col_start[:, None] + WIN_COLS)
    dc = jnp.clip(c[None, :] - qc[:, None] + (WIN_COLS - 1), 0, 2 * WIN_COLS - 2)
    full = jnp.where(in_win[None, None], rel_bias[:, :, dc], NEG)
    r = jnp.arange(WIN_ROWS, dtype=jnp.int32)
    variants = []
    for d0 in range(WIN_ROWS):
        band = full[:, d0 + r]
        variants.append(jnp.moveaxis(band, 1, 2).reshape(C_HEADS, GRID_W, WIN_ROWS * GRID_W))
    meta_band = jnp.broadcast_to(variants[WIN_ROWS - 1][:, :1], (C_HEADS, GRID_W, WIN_ROWS * GRID_W))
    variants.append(meta_band)
    band = jnp.stack(variants)
    mcol = jnp.concatenate([jnp.full((C_HEADS, HEAD_PAD), NEG, F32), meta_bias.astype(F32)], axis=-1)
    mcol = jnp.broadcast_to(mcol[None, :, None, :], (WIN_ROWS + 1, C_HEADS, GRID_W, BLK))
    return jnp.concatenate([mcol, band], axis=-1)


def _na_row_start(j, rows):
    return jnp.clip(j - 1 - WIN_ROWS // 2, 0, rows - WIN_ROWS)


def _na_kernel(q_ref, meta_ref, *rest, rows):
    band_refs = rest[:WIN_ROWS]
    bias_ref, o_ref = rest[WIN_ROWS:]
    j = jnp.minimum(pl.program_id(1), rows)
    variant = jnp.where(j == 0, NA_META_VARIANT, _na_row_start(j, rows) - (j - 1) + (WIN_ROWS - 1))
    kv = jnp.concatenate([meta_ref[...]] + [r[...] for r in band_refs], axis=0)
    for h in range(C_HEADS):
        lo, hi = h * HEAD_DIM, (h + 1) * HEAD_DIM
        s = lax.dot_general(q_ref[:, lo:hi], kv[:, lo:hi], (((1,), (1,)), ((), ())), preferred_element_type=F32)
        s = s + bias_ref[variant, h]
        e = jnp.exp(s - jnp.max(s, axis=-1, keepdims=True))
        p = e / jnp.sum(e, axis=-1, keepdims=True)
        o = jnp.dot(p.astype(BF16), kv[:, C_W + lo:C_W + hi], preferred_element_type=F32)
        o_ref[:, lo:hi] = o.astype(o_ref.dtype)


def neighbourhood_attention(qc, kvc, bias_table, grp):
    nblk = grp.lp // BLK
    rows = grp.rows

    def band_map(r):
        def index_map(b, j):
            jj = jnp.minimum(j, rows)
            return (b * nblk + 1 + _na_row_start(jj, rows) + r, 0)
        return index_map

    in_specs = [
        pl.BlockSpec((BLK, C_W), lambda b, j: (b * nblk + j, 0)),
        pl.BlockSpec((BLK, 2 * C_W), lambda b, j: (b * nblk, 0)),
    ]
    in_specs += [pl.BlockSpec((BLK, 2 * C_W), band_map(r)) for r in range(WIN_ROWS)]
    in_specs += [pl.BlockSpec(bias_table.shape, lambda b, j: (0, 0, 0, 0))]
    return pl.pallas_call(
        functools.partial(_na_kernel, rows=rows),
        grid=(grp.bsz, nblk),
        in_specs=in_specs,
        out_specs=pl.BlockSpec((BLK, C_W), lambda b, j: (b * nblk + j, 0)),
        out_shape=jax.ShapeDtypeStruct((grp.total, C_W), BF16),
        compiler_params=_params(("parallel", "parallel")),
        name="neighbourhood_attention",
    )(qc, kvc, *([kvc] * WIN_ROWS), bias_table)


HALO = 8
GATE_G = 0
GATE_BETA = 2 * B_HEADS


def _dn_prep_kernel(xp_ref, xc_ref, xn_ref, gate_ref, w_ref, a_ref, dtb_ref, qkv_ref, gb_ref, win_ref,
                    *, tb, valid_end):
    i = pl.program_id(1)
    part = pl.program_id(2)
    pos_w = i * tb - HALO + lax.broadcasted_iota(jnp.int32, (tb + 2 * HALO, 1), 0)
    ok_w = (pos_w >= HEAD_PAD) & (pos_w < valid_end)
    win_ref[0:HALO] = xp_ref[...]
    win_ref[HALO:HALO + tb] = xc_ref[...]
    win_ref[HALO + tb:] = xn_ref[...]
    win_ref[...] = jnp.where(ok_w, win_ref[...], 0.0)
    y = jnp.zeros((tb, B_W), F32)
    for j in range(CONV_K):
        y = y + win_ref[pl.ds(HALO - CONV_K // 2 + j, tb), :] * w_ref[j:j + 1, :]
    y = y * jax.nn.sigmoid(y)
    pos = i * tb + lax.broadcasted_iota(jnp.int32, (tb, 1), 0)
    ok = (pos >= HEAD_PAD) & (pos < valid_end)
    unit = jnp.where(part == 0, HEAD_DIM ** -0.5, 1.0)
    for h in range(B_HEADS):
        yh = y[:, h * HEAD_DIM:(h + 1) * HEAD_DIM]
        inv = lax.rsqrt(jnp.sum(yh * yh, axis=-1, keepdims=True) + EPS) * unit
        fac = jnp.where(part == 2, 1.0, inv)
        qkv_ref[:, h * HEAD_DIM:(h + 1) * HEAD_DIM] = jnp.where(ok, yh * fac, 0.0)

    @pl.when(part == 0)
    def _():
        x = gate_ref[...]
        lane = lax.broadcasted_iota(jnp.int32, x.shape, 1)
        z = x + dtb_ref[...]
        softplus = jnp.maximum(z, 0.0) + jnp.log1p(jnp.exp(-jnp.abs(z)))
        g = -jnp.exp(a_ref[...]) * softplus
        beta = jax.nn.sigmoid(x)
        out = jnp.where(lane < GATE_BETA, g, jnp.where(lane < 2 * GATE_BETA, beta, 0.0))
        gb_ref[...] = jnp.where(ok, out, 0.0)


def dn_prep(proj, grp, conv_w, a_log, dt_bias):
    tb = 128
    nb = grp.lp // tb
    base = grp.base // tb
    hb = tb // HALO
    last_halo = proj.shape[0] // HALO - 1
    qkv_col = COL_BQKV // B_W
    pad = jnp.zeros((1, 128 - 2 * B_HEADS), F32)
    a_row = jnp.concatenate([a_log.astype(F32).reshape(1, 2 * B_HEADS), pad], axis=-1)
    dtb_row = jnp.concatenate([dt_bias.astype(F32).reshape(1, 2 * B_HEADS), pad], axis=-1)
    return pl.pallas_call(
        functools.partial(_dn_prep_kernel, tb=tb, valid_end=grp.valid_end),
        grid=(grp.bsz, nb, 3),
        in_specs=[
            pl.BlockSpec((HALO, B_W), lambda b, i, p: (jnp.maximum((base + b * nb + i) * hb - 1, 0), qkv_col + p)),
            pl.BlockSpec((tb, B_W), lambda b, i, p: (base + b * nb + i, qkv_col + p)),
            pl.BlockSpec((HALO, B_W), lambda b, i, p: (jnp.minimum((base + b * nb + i + 1) * hb, last_halo), qkv_col + p)),
            pl.BlockSpec((tb, 128), lambda b, i, p: (base + b * nb + i, COL_GATE // 128)),
            pl.BlockSpec((CONV_K, B_W), lambda b, i, p: (0, p)),
            pl.BlockSpec((1, 128), lambda b, i, p: (0, 0)),
            pl.BlockSpec((1, 128), lambda b, i, p: (0, 0)),
        ],
        out_specs=[
            pl.BlockSpec((tb, B_W), lambda b, i, p: (b * nb + i, p)),
            pl.BlockSpec((tb, 128), lambda b, i, p: (b * nb + i, 0)),
        ],
        out_shape=[
            jax.ShapeDtypeStruct((grp.total, 3 * B_W), F32),
            jax.ShapeDtypeStruct((grp.total, 128), F32),
        ],
        scratch_shapes=[pltpu.VMEM((tb + 2 * HALO, B_W), F32)],
        compiler_params=_params(("parallel", "parallel", "arbitrary")),
        name="dn_prep",
    )(proj, proj, proj, proj, conv_w, a_row, dtb_row)


def _nt(a, b):
    return lax.dot_general(a, b, (((1,), (1,)), ((), ())), preferred_element_type=F32)


def _dn_chain(q, k, v, gcc, gcr, g_tot, beta, s_ref, idx, incl, strict):
    decay = jnp.exp(jnp.where(incl, gcc - gcr, -jnp.inf))
    kb = k * beta
    kb16 = kb.astype(BF16)
    k16 = k.astype(BF16)
    x = -jnp.where(strict, _nt(kb16, k16) * decay, 0.0)
    eye = (lax.broadcasted_iota(jnp.int32, (CHUNK, CHUNK), 0) == lax.broadcasted_iota(jnp.int32, (CHUNK, CHUNK), 1))
    t = x + eye.astype(F32)
    p = x
    for _ in range(5):
        p16 = p.astype(BF16)
        p = jnp.dot(p16, p16, preferred_element_type=F32)
        t = t + jnp.dot(t.astype(BF16), p.astype(BF16), preferred_element_type=F32)
    egc = jnp.exp(gcc)
    rhs = jnp.concatenate([kb * egc, v * beta], axis=-1).astype(BF16)
    wu = jnp.dot(t.astype(BF16), rhs, preferred_element_type=F32)
    w, u = wu[:, :HEAD_DIM], wu[:, HEAD_DIM:]
    qk = jnp.where(incl, _nt(q.astype(BF16), k16) * decay, 0.0)
    s = s_ref[idx]
    s16 = s.astype(BF16)
    lhs = jnp.concatenate([w, q * egc], axis=0).astype(BF16)
    both = jnp.dot(lhs, s16, preferred_element_type=F32)
    v_new = u - both[:CHUNK]
    v16 = v_new.astype(BF16)
    o = both[CHUNK:] + jnp.dot(qk.astype(BF16), v16, preferred_element_type=F32)
    kd = (k * jnp.exp(g_tot - gcc)).astype(BF16)
    s_ref[idx] = s * jnp.exp(g_tot) + lax.dot_general(kd, v16, (((0,), (0,)), ((), ())), preferred_element_type=F32)
    return o


def _dn_scan_kernel(qf_ref, kf_ref, vf_ref, gf_ref, qb_ref, kb_ref, vb_ref, gb_ref, of_ref, ob_ref, s_ref):
    @pl.when(pl.program_id(1) == 0)
    def _():
        s_ref[...] = jnp.zeros_like(s_ref)

    ri = lax.broadcasted_iota(jnp.int32, (CHUNK, CHUNK), 0)
    ci = lax.broadcasted_iota(jnp.int32, (CHUNK, CHUNK), 1)
    eye128 = (lax.broadcasted_iota(jnp.int32, (128, 128), 0) == lax.broadcasted_iota(jnp.int32, (128, 128), 1)).astype(F32)
    for d, (q_ref, k_ref, v_ref, g_ref, o_ref) in enumerate(
            ((qf_ref, kf_ref, vf_ref, gf_ref, of_ref), (qb_ref, kb_ref, vb_ref, gb_ref, ob_ref))):
        incl = (ri >= ci) if d == 0 else (ri <= ci)
        strict = (ri > ci) if d == 0 else (ri < ci)
        gates = g_ref[...]
        cum = jnp.dot(incl.astype(F32), gates, preferred_element_type=F32, precision=lax.Precision.HIGHEST)
        cum_t = lax.dot_general(eye128, cum, (((1,), (1,)), ((), ())), preferred_element_type=F32,
                                precision=lax.Precision.HIGHEST)
        last = CHUNK - 1 if d == 0 else 0
        for h in range(B_HEADS):
            c = GATE_G + d * B_HEADS + h
            lo, hi = h * HEAD_DIM, (h + 1) * HEAD_DIM
            gcc = cum[:, c:c + 1]
            gcr = cum_t[c:c + 1, :]
            g_tot = gcr[:, last:last + 1]
            beta = gates[:, GATE_BETA + c:GATE_BETA + c + 1]
            o = _dn_chain(q_ref[:, lo:hi], k_ref[:, lo:hi], v_ref[:, lo:hi], gcc, gcr, g_tot, beta,
                          s_ref, d * B_HEADS + h, incl, strict)
            o_ref[:, lo:hi] = o


def dn_scan(qkv, gb, grp):
    nc = grp.lp // BLK

    def fw(col):
        return lambda b, c: (b * nc + c, col)

    def bw(col):
        return lambda b, c: (b * nc + nc - 1 - c, col)

    blk = (BLK, B_W)
    return pl.pallas_call(
        _dn_scan_kernel,
        grid=(grp.bsz, nc),
        in_specs=[
            pl.BlockSpec(blk, fw(0)), pl.BlockSpec(blk, fw(1)), pl.BlockSpec(blk, fw(2)), pl.BlockSpec((BLK, 128), fw(0)),
            pl.BlockSpec(blk, bw(0)), pl.BlockSpec(blk, bw(1)), pl.BlockSpec(blk, bw(2)), pl.BlockSpec((BLK, 128), bw(0)),
        ],
        out_specs=[pl.BlockSpec(blk, fw(0)), pl.BlockSpec(blk, bw(0))],
        out_shape=[jax.ShapeDtypeStruct((grp.total, B_W), F32)] * 2,
        scratch_shapes=[pltpu.VMEM((2 * B_HEADS, HEAD_DIM, HEAD_DIM), F32)],
        compiler_params=_params(("parallel", "arbitrary")),
        name="dn_scan",
    )(qkv, qkv, qkv, gb, qkv, qkv, qkv, gb)


def _dn_post_kernel(of_ref, ob_ref, z_ref, gain_ref, o_ref):
    for h in range(B_HEADS):
        lo, hi = h * HEAD_DIM, (h + 1) * HEAD_DIM
        o = of_ref[:, lo:hi] + ob_ref[:, lo:hi]
        ms = jnp.mean(o * o, axis=-1, keepdims=True)
        z = z_ref[:, lo:hi]
        o_ref[:, lo:hi] = (o * lax.rsqrt(ms + EPS) * gain_ref[...] * (z * jax.nn.sigmoid(z))).astype(o_ref.dtype)


def dn_post(o_fw, o_bw, proj, grp, out_gain):
    tb = 128
    nb = grp.lp // tb
    base = grp.base // tb
    return pl.pallas_call(
        _dn_post_kernel,
        grid=(grp.bsz * nb,),
        in_specs=[
            pl.BlockSpec((tb, B_W), lambda i: (i, 0)),
            pl.BlockSpec((tb, B_W), lambda i: (i, 0)),
            pl.BlockSpec((tb, B_W), lambda i: (base + i, COL_BZ // B_W)),
            pl.BlockSpec((1, HEAD_DIM), lambda i: (0, 0)),
        ],
        out_specs=pl.BlockSpec((tb, B_W), lambda i: (i, 0)),
        out_shape=jax.ShapeDtypeStruct((grp.total, B_W), BF16),
        compiler_params=_params(("parallel",)),
        name="dn_post",
    )(o_fw, o_bw, proj, out_gain.reshape(1, HEAD_DIM))


def _permute_w_in(w_in):
    sizes = (A_Q_W, A_KV_W, A_KV_W, 3 * B_W, B_W, 2 * B_HEADS, 2 * B_HEADS, C_W, C_W, C_W)
    offs = [0]
    for s in sizes:
        offs.append(offs[-1] + s)
    seg = [w_in[..., offs[n]:offs[n + 1]] for n in range(len(sizes))]
    aq, ak, av, bqkv, bz, ba, bb, cq, ck, cv = seg
    used = COL_GATE + 4 * B_HEADS
    pad = jnp.zeros(w_in.shape[:-1] + (IN_WIDTH_PAD - used,), w_in.dtype)
    return jnp.concatenate([aq, ak, av, bqkv, bz, cq, ck, cv, ba, bb, pad], axis=-1).astype(BF16)


def _stack_tokens(xs, groups, meta_tokens):
    parts = []
    for x, grp in zip(xs, groups):
        d = x.shape[-1]
        meta = jnp.broadcast_to(meta_tokens.astype(x.dtype)[None], (grp.bsz, N_META, d))
        front = jnp.zeros((grp.bsz, HEAD_PAD, d), x.dtype)
        tail = jnp.zeros((grp.bsz, grp.lp - grp.valid_end, d), x.dtype)
        parts.append(jnp.concatenate([front, meta, x, tail], axis=1).reshape(grp.total, d))
    return jnp.concatenate(parts, axis=0)


def kernel(x_prompt, x_sample, meta_tokens, ffn1_norm, ffn1_w_gate, ffn1_w_up, ffn1_w_down, mix_norm, w_in,
           attn_q_norm, attn_k_norm, dn_conv_w, dn_a_log, dn_dt_bias, dn_out_norm, na_rel_bias, na_meta_bias,
           w_out, ffn2_norm, ffn2_w_gate, ffn2_w_up, ffn2_w_down, final_norm):
    depth = w_in.shape[0]
    g_prompt = Group(x_prompt.shape[0], x_prompt.shape[1], 0)
    g_sample = Group(x_sample.shape[0], x_sample.shape[1], g_prompt.total)
    groups = (g_prompt, g_sample)
    x = _stack_tokens((x_prompt, x_sample), groups, meta_tokens)
    tables = [_position_tables(grp) for grp in groups]

    w1g, w1u, w1d = (w.astype(BF16) for w in (ffn1_w_gate, ffn1_w_up, ffn1_w_down))
    w2g, w2u, w2d = (w.astype(BF16) for w in (ffn2_w_gate, ffn2_w_up, ffn2_w_down))
    w_in_p = _permute_w_in(w_in)
    w_out_b = w_out.astype(BF16)

    for l in range(depth):
        h = rmsnorm(x, ffn1_norm[l], BF16)
        x = residual_matmul(x, gate_up(h, w1g, w1u, l), w1d, l, 0.5)

        h = rmsnorm(x, mix_norm[l], BF16)
        proj = matmul(h, w_in_p, l, IN_TILE, F32)
        bias_table = _na_bias_table(na_rel_bias[l], na_meta_bias[l])
        mixed = []
        for grp, (cos, sin, key_bias) in zip(groups, tables):
            qa, ka, va, qc, kvc = prep_ac(proj, grp, cos, sin, attn_q_norm[l], attn_k_norm[l])
            o_a = flash_attention(qa, ka, va, key_bias, grp)
            o_c = neighbourhood_attention(qc, kvc, bias_table, grp)
            qkv, gb = dn_prep(proj, grp, dn_conv_w[l], dn_a_log[l], dn_dt_bias[l])
            o_fw, o_bw = dn_scan(qkv, gb, grp)
            o_b = dn_post(o_fw, o_bw, proj, grp, dn_out_norm[l])
            mixed.append(jnp.concatenate([o_a, o_b, o_c], axis=-1))
        x = residual_matmul(x, jnp.concatenate(mixed, axis=0), w_out_b, l, 1.0)

        h = rmsnorm(x, ffn2_norm[l], BF16)
        x = residual_matmul(x, gate_up(h, w2g, w2u, l), w2d, l, 0.5)

    return tuple(final_rmsnorm(x, final_norm, grp) for grp in groups)
```

```python
import functools

import jax
import jax.numpy as jnp
from jax import lax
from jax.experimental import pallas as pl
from jax.experimental.pallas import tpu as pltpu

F32 = jnp.float32
BF16 = jnp.bfloat16

HEAD_DIM = 128
N_META = 16
GRID_W = 64
EPS = 1e-6
A_HEADS = 4
A_KV_HEADS = 2
ROPE_THETA = 10000.0
B_HEADS = 8
CONV_K = 5
CHUNK = 64
C_HEADS = 4
WIN_ROWS = 8
WIN_COLS = 16

A_Q_W = A_HEADS * HEAD_DIM
A_KV_W = A_KV_HEADS * HEAD_DIM
B_W = B_HEADS * HEAD_DIM
C_W = C_HEADS * HEAD_DIM
MIX_WIDTH = A_Q_W + B_W + C_W

BLK = 64
HEAD_PAD = BLK - N_META
SEQ_ALIGN = 128
NEG = -1e30
LOG2E = 1.4426950408889634

COL_AQ = 0
COL_AK = COL_AQ + A_Q_W
COL_AV = COL_AK + A_KV_W
COL_BQKV = COL_AV + A_KV_W
COL_BZ = COL_BQKV + 3 * B_W
COL_CQ = COL_BZ + B_W
COL_CK = COL_CQ + C_W
COL_CV = COL_CK + C_W
COL_GATE = COL_CV + C_W
IN_TILE = 768
IN_WIDTH_PAD = -(-(COL_GATE + 128) // IN_TILE) * IN_TILE

V7X_VMEM_BYTES = 64 * 1024 * 1024
VMEM_LIMIT = 52 * 1024 * 1024


def _params(sem):
    return pltpu.CompilerParams(dimension_semantics=sem, vmem_limit_bytes=VMEM_LIMIT)


def _pick_tile(n, target, mult):
    best = None
    for t in range(mult, min(n, target) + 1, mult):
        if n % t == 0:
            best = t
    assert best is not None, (n, target, mult)
    return best


def _rmsnorm_kernel(x_ref, g_ref, o_ref):
    x = x_ref[...]
    ms = jnp.mean(x * x, axis=-1, keepdims=True)
    o_ref[...] = (x * lax.rsqrt(ms + EPS) * g_ref[...]).astype(o_ref.dtype)


def _norm_rows(x_ref, g_ref, h_ref):
    @pl.when(pl.program_id(1) == 0)
    def _():
        rows = x_ref.shape[0]
        sub = _pick_tile(rows, 128, 16)

        def body(r, carry):
            rs = pl.ds(pl.multiple_of(r * sub, sub), sub)
            x = x_ref[rs, :]
            ms = jnp.mean(x * x, axis=-1, keepdims=True)
            h_ref[rs, :] = (x * lax.rsqrt(ms + EPS) * g_ref[...]).astype(h_ref.dtype)
            return carry

        lax.fori_loop(0, rows // sub, body, 0)


def _norm_gate_up_kernel(x_ref, g_ref, wg_ref, wu_ref, o_ref, h_ref):
    _norm_rows(x_ref, g_ref, h_ref)
    h = h_ref[...]
    g = jnp.dot(h, wg_ref[...], preferred_element_type=F32)
    u = jnp.dot(h, wu_ref[...], preferred_element_type=F32)
    o_ref[...] = (g * jax.nn.sigmoid(g) * u).astype(o_ref.dtype)


def norm_gate_up(x, gain, wg, wu, layer):
    t, d = x.shape
    f = wg.shape[-1]
    tm = _pick_tile(t, 640, 16)
    tn = _pick_tile(f, 512, 128)
    return pl.pallas_call(
        _norm_gate_up_kernel,
        grid=(t // tm, f // tn),
        in_specs=[
            pl.BlockSpec((tm, d), lambda i, j: (i, 0)),
            pl.BlockSpec((1, d), lambda i, j: (0, 0)),
            pl.BlockSpec((None, d, tn), lambda i, j: (layer, 0, j)),
            pl.BlockSpec((None, d, tn), lambda i, j: (layer, 0, j)),
        ],
        out_specs=pl.BlockSpec((tm, tn), lambda i, j: (i, j)),
        out_shape=jax.ShapeDtypeStruct((t, f), BF16),
        scratch_shapes=[pltpu.VMEM((tm, d), BF16)],
        compiler_params=_params(("parallel", "arbitrary")),
        name="norm_gate_up",
    )(x, gain.reshape(1, d), wg, wu)


def _norm_matmul_kernel(x_ref, g_ref, w_ref, o_ref, h_ref):
    _norm_rows(x_ref, g_ref, h_ref)
    o_ref[...] = jnp.dot(h_ref[...], w_ref[...], preferred_element_type=F32).astype(o_ref.dtype)


def norm_matmul(x, gain, w, layer, tn_target, out_dtype):
    t, d = x.shape
    n = w.shape[-1]
    tm = _pick_tile(t, 640, 16)
    tn = _pick_tile(n, tn_target, 128)
    return pl.pallas_call(
        _norm_matmul_kernel,
        grid=(t // tm, n // tn),
        in_specs=[
            pl.BlockSpec((tm, d), lambda i, j: (i, 0)),
            pl.BlockSpec((1, d), lambda i, j: (0, 0)),
            pl.BlockSpec((None, d, tn), lambda i, j: (layer, 0, j)),
        ],
        out_specs=pl.BlockSpec((tm, tn), lambda i, j: (i, j)),
        out_shape=jax.ShapeDtypeStruct((t, n), out_dtype),
        scratch_shapes=[pltpu.VMEM((tm, d), BF16)],
        compiler_params=_params(("parallel", "arbitrary")),
        name="norm_matmul",
    )(x, gain.reshape(1, d), w)


def _residual_matmul_kernel(x_ref, a_ref, w_ref, o_ref, *, scale):
    o_ref[...] = x_ref[...] + scale * jnp.dot(a_ref[...], w_ref[...], preferred_element_type=F32)


def residual_matmul(x, a, w, layer, scale):
    t, d = x.shape
    k = a.shape[-1]
    tm = _pick_tile(t, 640, 16)
    tn = _pick_tile(d, 1024, 128)
    return pl.pallas_call(
        functools.partial(_residual_matmul_kernel, scale=scale),
        grid=(d // tn, t // tm),
        in_specs=[
            pl.BlockSpec((tm, tn), lambda j, i: (i, j)),
            pl.BlockSpec((tm, k), lambda j, i: (i, 0)),
            pl.BlockSpec((None, k, tn), lambda j, i: (layer, 0, j)),
        ],
        out_specs=pl.BlockSpec((tm, tn), lambda j, i: (i, j)),
        out_shape=jax.ShapeDtypeStruct((t, d), F32),
        compiler_params=_params(("parallel", "parallel")),
        name="residual_matmul",
    )(x, a, w)


def final_rmsnorm(x, gain, grp):
    d = x.shape[-1]
    nblk = grp.lp // BLK
    base = grp.base // BLK
    rows = grp.n_tok // BLK
    return pl.pallas_call(
        _rmsnorm_kernel,
        grid=(grp.bsz, rows),
        in_specs=[
            pl.BlockSpec((BLK, d), lambda b, i: (base + b * nblk + 1 + i, 0)),
            pl.BlockSpec((1, d), lambda b, i: (0, 0)),
        ],
        out_specs=pl.BlockSpec((None, BLK, d), lambda b, i: (b, i, 0)),
        out_shape=jax.ShapeDtypeStruct((grp.bsz, grp.n_tok, d), F32),
        compiler_params=_params(("parallel", "parallel")),
        name="final_norm",
    )(x, gain.reshape(1, d))


class Group:
    def __init__(self, bsz, n_tok, base):
        assert n_tok % GRID_W == 0 and n_tok // GRID_W >= WIN_ROWS
        self.bsz = bsz
        self.n_tok = n_tok
        self.lp = -(-(BLK + n_tok) // SEQ_ALIGN) * SEQ_ALIGN
        self.base = base
        self.rows = n_tok // GRID_W
        self.valid_end = BLK + n_tok

    @property
    def total(self):
        return self.bsz * self.lp


def _position_tables(grp):
    p = jnp.arange(grp.lp, dtype=jnp.int32)
    t = jnp.clip(p - BLK, 0, grp.n_tok - 1)
    is_tok = (p >= BLK) & (p < grp.valid_end)
    row = jnp.where(is_tok, t // GRID_W, 0).astype(F32)
    col = jnp.where(is_tok, t % GRID_W, 0).astype(F32)
    half = HEAD_DIM // 4
    freqs = ROPE_THETA ** (-jnp.arange(half, dtype=F32) / half)
    ang_r = row[:, None] * freqs[None, :]
    ang_c = col[:, None] * freqs[None, :]
    cos = jnp.concatenate([jnp.cos(ang_r), jnp.cos(ang_r), jnp.cos(ang_c), jnp.cos(ang_c)], axis=-1)
    sin = jnp.concatenate([-jnp.sin(ang_r), jnp.sin(ang_r), -jnp.sin(ang_c), jnp.sin(ang_c)], axis=-1)
    pk = jnp.arange(-(-grp.lp // FLASH_CK) * FLASH_CK, dtype=jnp.int32)
    key_bias = jnp.where((pk >= HEAD_PAD) & (pk < grp.valid_end), 0.0, NEG).astype(F32)
    return cos, sin, key_bias


def _swap_halves(x):
    lane = lax.broadcasted_iota(jnp.int32, x.shape, 1)
    return jnp.where(lane % 64 < 32, pltpu.roll(x, 96, 1), pltpu.roll(x, 32, 1))


def _prep_ac_kernel(a_ref, cq_ref, ck_ref, cv_ref, cos_ref, sin_ref, qg_ref, kg_ref,
                    qa_ref, ka_ref, va_ref, qc_ref, kvc_ref):
    cos = cos_ref[...]
    sin = sin_ref[...]
    scale = HEAD_DIM ** -0.5

    def norm_rope(x, gain):
        ms = jnp.mean(x * x, axis=-1, keepdims=True)
        y = x * lax.rsqrt(ms + EPS) * gain
        return y * cos + _swap_halves(y) * sin

    for h in range(A_HEADS):
        x = a_ref[:, COL_AQ + h * HEAD_DIM:COL_AQ + (h + 1) * HEAD_DIM]
        qa_ref[:, h * HEAD_DIM:(h + 1) * HEAD_DIM] = (norm_rope(x, qg_ref[...]) * (scale * LOG2E)).astype(BF16)
    for h in range(A_KV_HEADS):
        x = a_ref[:, COL_AK + h * HEAD_DIM:COL_AK + (h + 1) * HEAD_DIM]
        ka_ref[:, h * HEAD_DIM:(h + 1) * HEAD_DIM] = norm_rope(x, kg_ref[...]).astype(BF16)
    va_ref[...] = a_ref[:, COL_AV:COL_AV + A_KV_W].astype(BF16)
    qc_ref[...] = (cq_ref[...] * scale).astype(BF16)
    kvc_ref[:, :C_W] = ck_ref[...].astype(BF16)
    kvc_ref[:, C_W:] = cv_ref[...].astype(BF16)


def prep_ac(proj, grp, cos, sin, q_gain, k_gain):
    tr = 128
    nb = grp.lp // tr
    base = grp.base // tr
    a_w = A_Q_W + 2 * A_KV_W

    def rowmap(col):
        return lambda b, i: (base + b * nb + i, col)

    out_map = lambda b, i: (b * nb + i, 0)
    n = grp.total
    return pl.pallas_call(
        _prep_ac_kernel,
        grid=(grp.bsz, nb),
        in_specs=[
            pl.BlockSpec((tr, a_w), rowmap(0)),
            pl.BlockSpec((tr, C_W), rowmap(COL_CQ // C_W)),
            pl.BlockSpec((tr, C_W), rowmap(COL_CK // C_W)),
            pl.BlockSpec((tr, C_W), rowmap(COL_CV // C_W)),
            pl.BlockSpec((tr, HEAD_DIM), lambda b, i: (i, 0)),
            pl.BlockSpec((tr, HEAD_DIM), lambda b, i: (i, 0)),
            pl.BlockSpec((1, HEAD_DIM), lambda b, i: (0, 0)),
            pl.BlockSpec((1, HEAD_DIM), lambda b, i: (0, 0)),
        ],
        out_specs=[
            pl.BlockSpec((tr, A_Q_W), out_map),
            pl.BlockSpec((tr, A_KV_W), out_map),
            pl.BlockSpec((tr, A_KV_W), out_map),
            pl.BlockSpec((tr, C_W), out_map),
            pl.BlockSpec((tr, 2 * C_W), out_map),
        ],
        out_shape=[
            jax.ShapeDtypeStruct((n, A_Q_W), BF16),
            jax.ShapeDtypeStruct((n, A_KV_W), BF16),
            jax.ShapeDtypeStruct((n, A_KV_W), BF16),
            jax.ShapeDtypeStruct((n, C_W), BF16),
            jax.ShapeDtypeStruct((n, 2 * C_W), BF16),
        ],
        compiler_params=_params(("parallel", "parallel")),
        name="prep_ac",
    )(proj, proj, proj, proj, cos, sin, q_gain.reshape(1, HEAD_DIM), k_gain.reshape(1, HEAD_DIM))


FLASH_CK = 256
FLASH_SUB = 128


def _flash_kernel(q_ref, k_ref, v_ref, bias_ref, o_ref, q2_ref, s_ref, m_ref, l_ref, acc_ref, *, tq, nchunks):
    q2_ref[:tq] = q_ref[:, :HEAD_DIM]
    q2_ref[tq:] = q_ref[:, HEAD_DIM:]
    q2 = q2_ref[...]

    def scores(t, slot):
        start = pl.multiple_of(t * FLASH_CK, FLASH_CK)
        k = k_ref[pl.ds(start, FLASH_CK), :]
        s_ref[slot] = lax.dot_general(q2, k, (((1,), (1,)), ((), ())), preferred_element_type=F32)

    def step(t, slot, masked, prefetch):
        if prefetch:
            scores(t + 1, 1 - slot)

        def read_scores():
            s = s_ref[slot]
            return s + bias_ref[pl.ds(t, 1), :] if masked else s

        m = m_ref[...]
        m_new = jnp.maximum(m, jnp.max(read_scores(), axis=-1, keepdims=True))
        m_ref[...] = m_new
        p = jnp.exp2(read_scores() - jnp.concatenate([m_new, m_new], axis=-1))
        alpha = jnp.exp2(m - m_new)
        l_ref[...] = alpha * l_ref[...] + jnp.sum(p, axis=-1, keepdims=True)
        v = v_ref[pl.ds(pl.multiple_of(t * FLASH_CK, FLASH_CK), FLASH_CK), :]
        acc_ref[...] = alpha * acc_ref[...] + jnp.dot(p.astype(BF16), v, preferred_element_type=F32)

    m_ref[...] = jnp.full(m_ref.shape, -jnp.inf, F32)
    l_ref[...] = jnp.zeros(l_ref.shape, F32)
    acc_ref[...] = jnp.zeros(acc_ref.shape, F32)
    scores(0, 0)
    step(0, 0, True, nchunks > 1)
    if nchunks > 1:
        t0 = 1
        if (nchunks - 2) % 2 == 1:
            step(1, 1, False, True)
            t0 = 2

        def pair(j, carry):
            t = t0 + 2 * j
            step(t, t0 % 2, False, True)
            step(t + 1, 1 - t0 % 2, False, True)
            return carry

        lax.fori_loop(0, (nchunks - 1 - t0) // 2, pair, 0)
        step(nchunks - 1, (nchunks - 1) % 2, True, False)
    o = acc_ref[...] / l_ref[...]
    o_ref[:, :HEAD_DIM] = o[:tq].astype(o_ref.dtype)
    o_ref[:, HEAD_DIM:] = o[tq:].astype(o_ref.dtype)


def flash_attention(qa, ka, va, key_bias, grp):
    lp = grp.lp
    lkv = ka.shape[0] // grp.bsz
    assert lkv % FLASH_CK == 0 and lkv - grp.valid_end <= FLASH_CK and HEAD_PAD <= FLASH_CK
    tq = _pick_tile(lp, 384, 128)
    nchunks = lkv // FLASH_CK
    nq = lp // tq
    gw = 2 * HEAD_DIM
    return pl.pallas_call(
        functools.partial(_flash_kernel, tq=tq, nchunks=nchunks),
        grid=(grp.bsz, A_KV_HEADS, nq),
        in_specs=[
            pl.BlockSpec((tq, gw), lambda b, g, i: (b * nq + i, g)),
            pl.BlockSpec((lkv, HEAD_DIM), lambda b, g, i: (b, g)),
            pl.BlockSpec((lkv, HEAD_DIM), lambda b, g, i: (b, g)),
            pl.BlockSpec((nchunks, FLASH_CK), lambda b, g, i: (0, 0)),
        ],
        out_specs=pl.BlockSpec((tq, gw), lambda b, g, i: (b * nq + i, g)),
        out_shape=jax.ShapeDtypeStruct((grp.total, A_Q_W), BF16),
        scratch_shapes=[
            pltpu.VMEM((2 * tq, HEAD_DIM), BF16),
            pltpu.VMEM((2, 2 * tq, FLASH_CK), F32),
            pltpu.VMEM((2 * tq, HEAD_DIM), F32),
            pltpu.VMEM((2 * tq, HEAD_DIM), F32),
            pltpu.VMEM((2 * tq, HEAD_DIM), F32),
        ],
        compiler_params=_params(("parallel", "parallel", "parallel")),
        name="flash_attention",
    )(qa, ka, va, key_bias.reshape(nchunks, FLASH_CK))


def pad_keys(x, grp, value=0):
    lkv = -(-grp.lp // FLASH_CK) * FLASH_CK
    x = x.reshape(grp.bsz, grp.lp, -1)
    x = jnp.pad(x, ((0, 0), (0, lkv - grp.lp), (0, 0)), constant_values=value)
    return x.reshape(grp.bsz * lkv, -1)


NA_KEYS = BLK + WIN_ROWS * GRID_W
NA_META_VARIANT = WIN_ROWS


def _na_bias_table(rel_bias, meta_bias):
    rel_bias = rel_bias.astype(F32)
    qc = jnp.arange(GRID_W, dtype=jnp.int32)
    c = jnp.arange(GRID_W, dtype=jnp.int32)
    col_start = jnp.clip(qc - WIN_COLS // 2, 0, GRID_W - WIN_COLS)
    in_win = (c[None, :] >= col_start[:, None]) & (c[None, :] < col_start[:, None] + WIN_COLS)
    dc = jnp.clip(c[None, :] - qc[:, None] + (WIN_COLS - 1), 0, 2 * WIN_COLS - 2)
    full = jnp.where(in_win[None, None], rel_bias[:, :, dc], NEG)
    r = jnp.arange(WIN_ROWS, dtype=jnp.int32)
    variants = []
    for d0 in range(WIN_ROWS):
        band = full[:, d0 + r]
        variants.append(jnp.moveaxis(band, 1, 2).reshape(C_HEADS, GRID_W, WIN_ROWS * GRID_W))
    meta_band = jnp.broadcast_to(variants[WIN_ROWS - 1][:, :1], (C_HEADS, GRID_W, WIN_ROWS * GRID_W))
    variants.append(meta_band)
    band = jnp.stack(variants)
    mcol = jnp.concatenate([jnp.full((C_HEADS, HEAD_PAD), NEG, F32), meta_bias.astype(F32)], axis=-1)
    mcol = jnp.broadcast_to(mcol[None, :, None, :], (WIN_ROWS + 1, C_HEADS, GRID_W, BLK))
    return jnp.concatenate([mcol, band], axis=-1)


def _na_row_start(j, rows):
    return jnp.clip(j - 1 - WIN_ROWS // 2, 0, rows - WIN_ROWS)


def _na_kernel(q_ref, meta_ref, *rest, rows):
    band_refs = rest[:WIN_ROWS]
    bias_ref, o_ref = rest[WIN_ROWS:]
    j = jnp.minimum(pl.program_id(1), rows)
    variant = jnp.where(j == 0, NA_META_VARIANT, _na_row_start(j, rows) - (j - 1) + (WIN_ROWS - 1))
    kv = jnp.concatenate([meta_ref[...]] + [r[...] for r in band_refs], axis=0)
    for h in range(C_HEADS):
        lo, hi = h * HEAD_DIM, (h + 1) * HEAD_DIM
        s = lax.dot_general(q_ref[:, lo:hi], kv[:, lo:hi], (((1,), (1,)), ((), ())), preferred_element_type=F32)
        s = s + bias_ref[variant, h]
        e = jnp.exp(s - jnp.max(s, axis=-1, keepdims=True))
        p = e / jnp.sum(e, axis=-1, keepdims=True)
        o = jnp.dot(p.astype(BF16), kv[:, C_W + lo:C_W + hi], preferred_element_type=F32)
        o_ref[:, lo:hi] = o.astype(o_ref.dtype)


def neighbourhood_attention(qc, kvc, bias_table, grp):
    nblk = grp.lp // BLK
    rows = grp.rows

    def band_map(r):
        def index_map(b, j):
            jj = jnp.minimum(j, rows)
            return (b * nblk + 1 + _na_row_start(jj, rows) + r, 0)
        return index_map

    in_specs = [
        pl.BlockSpec((BLK, C_W), lambda b, j: (b * nblk + j, 0)),
        pl.BlockSpec((BLK, 2 * C_W), lambda b, j: (b * nblk, 0)),
    ]
    in_specs += [pl.BlockSpec((BLK, 2 * C_W), band_map(r)) for r in range(WIN_ROWS)]
    in_specs += [pl.BlockSpec(bias_table.shape, lambda b, j: (0, 0, 0, 0))]
    return pl.pallas_call(
        functools.partial(_na_kernel, rows=rows),
        grid=(grp.bsz, nblk),
        in_specs=in_specs,
        out_specs=pl.BlockSpec((BLK, C_W), lambda b, j: (b * nblk + j, 0)),
        out_shape=jax.ShapeDtypeStruct((grp.total, C_W), BF16),
        compiler_params=_params(("parallel", "parallel")),
        name="neighbourhood_attention",
    )(qc, kvc, *([kvc] * WIN_ROWS), bias_table)


HALO = 8
GATE_G = 0
GATE_BETA = 2 * B_HEADS


def _dn_prep_kernel(xp_ref, xc_ref, xn_ref, gate_ref, w_ref, a_ref, dtb_ref, qkv_ref, gb_ref, win_ref,
                    *, tb, valid_end):
    i = pl.program_id(1)
    part = pl.program_id(2)
    pos_w = i * tb - HALO + lax.broadcasted_iota(jnp.int32, (tb + 2 * HALO, 1), 0)
    ok_w = (pos_w >= HEAD_PAD) & (pos_w < valid_end)
    win_ref[0:HALO] = xp_ref[...]
    win_ref[HALO:HALO + tb] = xc_ref[...]
    win_ref[HALO + tb:] = xn_ref[...]
    win_ref[...] = jnp.where(ok_w, win_ref[...], 0.0)
    y = jnp.zeros((tb, B_W), F32)
    for j in range(CONV_K):
        y = y + win_ref[pl.ds(HALO - CONV_K // 2 + j, tb), :] * w_ref[j:j + 1, :]
    y = y * jax.nn.sigmoid(y)
    pos = i * tb + lax.broadcasted_iota(jnp.int32, (tb, 1), 0)
    ok = (pos >= HEAD_PAD) & (pos < valid_end)
    unit = jnp.where(part == 0, HEAD_DIM ** -0.5, 1.0)
    for h in range(B_HEADS):
        yh = y[:, h * HEAD_DIM:(h + 1) * HEAD_DIM]
        inv = lax.rsqrt(jnp.sum(yh * yh, axis=-1, keepdims=True) + EPS) * unit
        fac = jnp.where(part == 2, 1.0, inv)
        qkv_ref[:, h * HEAD_DIM:(h + 1) * HEAD_DIM] = jnp.where(ok, yh * fac, 0.0)

    @pl.when(part == 0)
    def _():
        x = gate_ref[...]
        lane = lax.broadcasted_iota(jnp.int32, x.shape, 1)
        z = x + dtb_ref[...]
        softplus = jnp.maximum(z, 0.0) + jnp.log1p(jnp.exp(-jnp.abs(z)))
        g = -jnp.exp(a_ref[...]) * softplus
        beta = jax.nn.sigmoid(x)
        out = jnp.where(lane < GATE_BETA, g, jnp.where(lane < 2 * GATE_BETA, beta, 0.0))
        gb_ref[...] = jnp.where(ok, out, 0.0)


def dn_prep(proj, grp, conv_w, a_log, dt_bias):
    tb = 128
    nb = grp.lp // tb
    base = grp.base // tb
    hb = tb // HALO
    last_halo = proj.shape[0] // HALO - 1
    qkv_col = COL_BQKV // B_W
    pad = jnp.zeros((1, 128 - 2 * B_HEADS), F32)
    a_row = jnp.concatenate([a_log.astype(F32).reshape(1, 2 * B_HEADS), pad], axis=-1)
    dtb_row = jnp.concatenate([dt_bias.astype(F32).reshape(1, 2 * B_HEADS), pad], axis=-1)
    return pl.pallas_call(
        functools.partial(_dn_prep_kernel, tb=tb, valid_end=grp.valid_end),
        grid=(grp.bsz, nb, 3),
        in_specs=[
            pl.BlockSpec((HALO, B_W), lambda b, i, p: (jnp.maximum((base + b * nb + i) * hb - 1, 0), qkv_col + p)),
            pl.BlockSpec((tb, B_W), lambda b, i, p: (base + b * nb + i, qkv_col + p)),
            pl.BlockSpec((HALO, B_W), lambda b, i, p: (jnp.minimum((base + b * nb + i + 1) * hb, last_halo), qkv_col + p)),
            pl.BlockSpec((tb, 128), lambda b, i, p: (base + b * nb + i, COL_GATE // 128)),
            pl.BlockSpec((CONV_K, B_W), lambda b, i, p: (0, p)),
            pl.BlockSpec((1, 128), lambda b, i, p: (0, 0)),
            pl.BlockSpec((1, 128), lambda b, i, p: (0, 0)),
        ],
        out_specs=[
            pl.BlockSpec((tb, B_W), lambda b, i, p: (b * nb + i, p)),
            pl.BlockSpec((tb, 128), lambda b, i, p: (b * nb + i, 0)),
        ],
        out_shape=[
            jax.ShapeDtypeStruct((grp.total, 3 * B_W), F32),
            jax.ShapeDtypeStruct((grp.total, 128), F32),
        ],
        scratch_shapes=[pltpu.VMEM((tb + 2 * HALO, B_W), F32)],
        compiler_params=_params(("parallel", "parallel", "arbitrary")),
        name="dn_prep",
    )(proj, proj, proj, proj, conv_w, a_row, dtb_row)


def _nt(a, b):
    return lax.dot_general(a, b, (((1,), (1,)), ((), ())), preferred_element_type=F32)


def _dn_scan_kernel(qf_ref, kf_ref, vf_ref, gf_ref, qb_ref, kb_ref, vb_ref, gb_ref, of_ref, ob_ref, s_ref):
    @pl.when(pl.program_id(1) == 0)
    def _():
        s_ref[...] = jnp.zeros_like(s_ref)

    ri = lax.broadcasted_iota(jnp.int32, (CHUNK, CHUNK), 0)
    ci = lax.broadcasted_iota(jnp.int32, (CHUNK, CHUNK), 1)
    eye = (ri == ci).astype(F32)
    eye128 = (lax.broadcasted_iota(jnp.int32, (128, 128), 0) == lax.broadcasted_iota(jnp.int32, (128, 128), 1)).astype(F32)
    dirs = ((qf_ref, kf_ref, vf_ref, gf_ref, of_ref), (qb_ref, kb_ref, vb_ref, gb_ref, ob_ref))
    n = 2 * B_HEADS
    incl, strict, gcc, gcr, g_tot, beta, refs = [], [], [], [], [], [], []
    for d, (q_ref, k_ref, v_ref, g_ref, o_ref) in enumerate(dirs):
        incl_d = (ri >= ci) if d == 0 else (ri <= ci)
        strict_d = (ri > ci) if d == 0 else (ri < ci)
        gates = g_ref[...]
        cum = jnp.dot(incl_d.astype(F32), gates, preferred_element_type=F32, precision=lax.Precision.HIGHEST)
        cum_t = lax.dot_general(eye128, cum, (((1,), (1,)), ((), ())), preferred_element_type=F32,
                                precision=lax.Precision.HIGHEST)
        last = CHUNK - 1 if d == 0 else 0
        for h in range(B_HEADS):
            c = GATE_G + d * B_HEADS + h
            sl = slice(h * HEAD_DIM, (h + 1) * HEAD_DIM)
            incl.append(incl_d)
            strict.append(strict_d)
            gcc.append(cum[:, c:c + 1])
            gcr.append(cum_t[c:c + 1, :])
            g_tot.append(cum_t[c:c + 1, last:last + 1])
            beta.append(gates[:, GATE_BETA + c:GATE_BETA + c + 1])
            refs.append((q_ref, k_ref, v_ref, o_ref, sl))

    decay = [jnp.exp(jnp.where(incl[i], gcc[i] - gcr[i], -jnp.inf)) for i in range(n)]
    k = [refs[i][1][:, refs[i][4]] for i in range(n)]
    kb = [k[i] * beta[i] for i in range(n)]
    k16 = [k[i].astype(BF16) for i in range(n)]
    x = [-jnp.where(strict[i], _nt(kb[i].astype(BF16), k16[i]) * decay[i], 0.0) for i in range(n)]
    t = [x[i] + eye for i in range(n)]
    p = x
    for _ in range(5):
        p16 = [p[i].astype(BF16) for i in range(n)]
        p = [jnp.dot(p16[i], p16[i], preferred_element_type=F32) for i in range(n)]
        t = [t[i] + jnp.dot(t[i].astype(BF16), p[i].astype(BF16), preferred_element_type=F32) for i in range(n)]
    egc = [jnp.exp(gcc[i]) for i in range(n)]
    rhs = [jnp.concatenate([kb[i] * egc[i], refs[i][2][:, refs[i][4]] * beta[i]], axis=-1).astype(BF16) for i in range(n)]
    wu = [jnp.dot(t[i].astype(BF16), rhs[i], preferred_element_type=F32) for i in range(n)]
    q = [refs[i][0][:, refs[i][4]] for i in range(n)]
    qk = [jnp.where(incl[i], _nt(q[i].astype(BF16), k16[i]) * decay[i], 0.0).astype(BF16) for i in range(n)]
    s = [s_ref[i] for i in range(n)]
    lhs = [jnp.concatenate([wu[i][:, :HEAD_DIM], q[i] * egc[i]], axis=0).astype(BF16) for i in range(n)]
    both = [jnp.dot(lhs[i], s[i].astype(BF16), preferred_element_type=F32) for i in range(n)]
    v16 = [(wu[i][:, HEAD_DIM:] - both[i][:CHUNK]).astype(BF16) for i in range(n)]
    for i in range(n):
        o_ref, sl = refs[i][3], refs[i][4]
        o_ref[:, sl] = both[i][CHUNK:] + jnp.dot(qk[i], v16[i], preferred_element_type=F32)
    kd = [(k[i] * jnp.exp(g_tot[i] - gcc[i])).astype(BF16) for i in range(n)]
    for i in range(n):
        s_ref[i] = s[i] * jnp.exp(g_tot[i]) + lax.dot_general(
            kd[i], v16[i], (((0,), (0,)), ((), ())), preferred_element_type=F32)


def dn_scan(qkv, gb, grp):
    nc = grp.lp // BLK

    def fw(col):
        return lambda b, c: (b * nc + c, col)

    def bw(col):
        return lambda b, c: (b * nc + nc - 1 - c, col)

    blk = (BLK, B_W)
    return pl.pallas_call(
        _dn_scan_kernel,
        grid=(grp.bsz, nc),
        in_specs=[
            pl.BlockSpec(blk, fw(0)), pl.BlockSpec(blk, fw(1)), pl.BlockSpec(blk, fw(2)), pl.BlockSpec((BLK, 128), fw(0)),
            pl.BlockSpec(blk, bw(0)), pl.BlockSpec(blk, bw(1)), pl.BlockSpec(blk, bw(2)), pl.BlockSpec((BLK, 128), bw(0)),
        ],
        out_specs=[pl.BlockSpec(blk, fw(0)), pl.BlockSpec(blk, bw(0))],
        out_shape=[jax.ShapeDtypeStruct((grp.total, B_W), F32)] * 2,
        scratch_shapes=[pltpu.VMEM((2 * B_HEADS, HEAD_DIM, HEAD_DIM), F32)],
        compiler_params=_params(("parallel", "arbitrary")),
        name="dn_scan",
    )(qkv, qkv, qkv, gb, qkv, qkv, qkv, gb)


def _dn_post_kernel(of_ref, ob_ref, z_ref, gain_ref, o_ref):
    for h in range(B_HEADS):
        lo, hi = h * HEAD_DIM, (h + 1) * HEAD_DIM
        o = of_ref[:, lo:hi] + ob_ref[:, lo:hi]
        ms = jnp.mean(o * o, axis=-1, keepdims=True)
        z = z_ref[:, lo:hi]
        o_ref[:, lo:hi] = (o * lax.rsqrt(ms + EPS) * gain_ref[...] * (z * jax.nn.sigmoid(z))).astype(o_ref.dtype)


def dn_post(o_fw, o_bw, proj, grp, out_gain):
    tb = 128
    nb = grp.lp // tb
    base = grp.base // tb
    return pl.pallas_call(
        _dn_post_kernel,
        grid=(grp.bsz * nb,),
        in_specs=[
            pl.BlockSpec((tb, B_W), lambda i: (i, 0)),
            pl.BlockSpec((tb, B_W), lambda i: (i, 0)),
            pl.BlockSpec((tb, B_W), lambda i: (base + i, COL_BZ // B_W)),
            pl.BlockSpec((1, HEAD_DIM), lambda i: (0, 0)),
        ],
        out_specs=pl.BlockSpec((tb, B_W), lambda i: (i, 0)),
        out_shape=jax.ShapeDtypeStruct((grp.total, B_W), BF16),
        compiler_params=_params(("parallel",)),
        name="dn_post",
    )(o_fw, o_bw, proj, out_gain.reshape(1, HEAD_DIM))


def _permute_w_in(w_in):
    sizes = (A_Q_W, A_KV_W, A_KV_W, 3 * B_W, B_W, 2 * B_HEADS, 2 * B_HEADS, C_W, C_W, C_W)
    offs = [0]
    for s in sizes:
        offs.append(offs[-1] + s)
    seg = [w_in[..., offs[n]:offs[n + 1]] for n in range(len(sizes))]
    aq, ak, av, bqkv, bz, ba, bb, cq, ck, cv = seg
    used = COL_GATE + 4 * B_HEADS
    pad = jnp.zeros(w_in.shape[:-1] + (IN_WIDTH_PAD - used,), w_in.dtype)
    return jnp.concatenate([aq, ak, av, bqkv, bz, cq, ck, cv, ba, bb, pad], axis=-1).astype(BF16)


def _stack_tokens(xs, groups, meta_tokens):
    parts = []
    for x, grp in zip(xs, groups):
        d = x.shape[-1]
        meta = jnp.broadcast_to(meta_tokens.astype(x.dtype)[None], (grp.bsz, N_META, d))
        front = jnp.zeros((grp.bsz, HEAD_PAD, d), x.dtype)
        tail = jnp.zeros((grp.bsz, grp.lp - grp.valid_end, d), x.dtype)
        parts.append(jnp.concatenate([front, meta, x, tail], axis=1).reshape(grp.total, d))
    return jnp.concatenate(parts, axis=0)


def kernel(x_prompt, x_sample, meta_tokens, ffn1_norm, ffn1_w_gate, ffn1_w_up, ffn1_w_down, mix_norm, w_in,
           attn_q_norm, attn_k_norm, dn_conv_w, dn_a_log, dn_dt_bias, dn_out_norm, na_rel_bias, na_meta_bias,
           w_out, ffn2_norm, ffn2_w_gate, ffn2_w_up, ffn2_w_down, final_norm):
    depth = w_in.shape[0]
    g_prompt = Group(x_prompt.shape[0], x_prompt.shape[1], 0)
    g_sample = Group(x_sample.shape[0], x_sample.shape[1], g_prompt.total)
    groups = (g_prompt, g_sample)
    x = _stack_tokens((x_prompt, x_sample), groups, meta_tokens)
    tables = [_position_tables(grp) for grp in groups]

    w1g, w1u, w1d = (w.astype(BF16) for w in (ffn1_w_gate, ffn1_w_up, ffn1_w_down))
    w2g, w2u, w2d = (w.astype(BF16) for w in (ffn2_w_gate, ffn2_w_up, ffn2_w_down))
    w_in_p = _permute_w_in(w_in)
    w_out_b = w_out.astype(BF16)

    for l in range(depth):
        x = residual_matmul(x, norm_gate_up(x, ffn1_norm[l], w1g, w1u, l), w1d, l, 0.5)

        proj = norm_matmul(x, mix_norm[l], w_in_p, l, IN_TILE, F32)
        bias_table = _na_bias_table(na_rel_bias[l], na_meta_bias[l])
        mixed = []
        for grp, (cos, sin, key_bias) in zip(groups, tables):
            qa, ka, va, qc, kvc = prep_ac(proj, grp, cos, sin, attn_q_norm[l], attn_k_norm[l])
            o_a = flash_attention(qa, pad_keys(ka, grp), pad_keys(va, grp), key_bias, grp)
            o_c = neighbourhood_attention(qc, kvc, bias_table, grp)
            qkv, gb = dn_prep(proj, grp, dn_conv_w[l], dn_a_log[l], dn_dt_bias[l])
            o_fw, o_bw = dn_scan(qkv, gb, grp)
            o_b = dn_post(o_fw, o_bw, proj, grp, dn_out_norm[l])
            mixed.append(jnp.concatenate([o_a, o_b, o_c], axis=-1))
        x = residual_matmul(x, jnp.concatenate(mixed, axis=0), w_out_b, l, 1.0)

        x = residual_matmul(x, norm_gate_up(x, ffn2_norm[l], w2g, w2u, l), w2d, l, 0.5)

    return tuple(final_rmsnorm(x, final_norm, grp) for grp in groups)
```

```python
import functools

import jax
import jax.numpy as jnp
from jax import lax
from jax.experimental import pallas as pl
from jax.experimental.pallas import tpu as pltpu

F32 = jnp.float32
BF16 = jnp.bfloat16

HEAD_DIM = 128
N_META = 16
GRID_W = 64
EPS = 1e-6
A_HEADS = 4
A_KV_HEADS = 2
ROPE_THETA = 10000.0
B_HEADS = 8
CONV_K = 5
CHUNK = 64
C_HEADS = 4
WIN_ROWS = 8
WIN_COLS = 16

A_Q_W = A_HEADS * HEAD_DIM
A_KV_W = A_KV_HEADS * HEAD_DIM
B_W = B_HEADS * HEAD_DIM
C_W = C_HEADS * HEAD_DIM
MIX_WIDTH = A_Q_W + B_W + C_W

BLK = 64
HEAD_PAD = BLK - N_META
SEQ_ALIGN = 128
NEG = -1e30
LOG2E = 1.4426950408889634

COL_AQ = 0
COL_AK = COL_AQ + A_Q_W
COL_AV = COL_AK + A_KV_W
COL_BQKV = COL_AV + A_KV_W
COL_BZ = COL_BQKV + 3 * B_W
COL_CQ = COL_BZ + B_W
COL_CK = COL_CQ + C_W
COL_CV = COL_CK + C_W
COL_GATE = COL_CV + C_W
MIX_COL_B = 0
MIX_COL_A = MIX_COL_B + B_W
MIX_COL_C = MIX_COL_A + A_Q_W
IN_TILE = 768
IN_WIDTH_PAD = -(-(COL_GATE + 128) // IN_TILE) * IN_TILE

V7X_VMEM_BYTES = 64 * 1024 * 1024
VMEM_LIMIT = 52 * 1024 * 1024


def _params(sem):
    return pltpu.CompilerParams(dimension_semantics=sem, vmem_limit_bytes=VMEM_LIMIT)


def _pick_tile(n, target, mult):
    best = None
    for t in range(mult, min(n, target) + 1, mult):
        if n % t == 0:
            best = t
    assert best is not None, (n, target, mult)
    return best


def _rmsnorm_kernel(x_ref, g_ref, o_ref):
    x = x_ref[...]
    ms = jnp.mean(x * x, axis=-1, keepdims=True)
    o_ref[...] = (x * lax.rsqrt(ms + EPS) * g_ref[...]).astype(o_ref.dtype)


def _norm_rows(x_ref, g_ref, h_ref):
    @pl.when(pl.program_id(1) == 0)
    def _():
        rows = x_ref.shape[0]
        sub = _pick_tile(rows, 128, 16)

        def body(r, carry):
            rs = pl.ds(pl.multiple_of(r * sub, sub), sub)
            x = x_ref[rs, :]
            ms = jnp.mean(x * x, axis=-1, keepdims=True)
            h_ref[rs, :] = (x * lax.rsqrt(ms + EPS) * g_ref[...]).astype(h_ref.dtype)
            return carry

        lax.fori_loop(0, rows // sub, body, 0)


def _norm_gate_up_kernel(x_ref, g_ref, wg_ref, wu_ref, o_ref, h_ref):
    _norm_rows(x_ref, g_ref, h_ref)
    h = h_ref[...]
    g = jnp.dot(h, wg_ref[...], preferred_element_type=F32)
    u = jnp.dot(h, wu_ref[...], preferred_element_type=F32)
    o_ref[...] = (g * jax.nn.sigmoid(g) * u).astype(o_ref.dtype)


def norm_gate_up(x, gain, wg, wu, layer):
    t, d = x.shape
    f = wg.shape[-1]
    tm = _pick_tile(t, 640, 16)
    tn = _pick_tile(f, 512, 128)
    return pl.pallas_call(
        _norm_gate_up_kernel,
        grid=(t // tm, f // tn),
        in_specs=[
            pl.BlockSpec((tm, d), lambda i, j: (i, 0)),
            pl.BlockSpec((1, d), lambda i, j: (0, 0)),
            pl.BlockSpec((None, d, tn), lambda i, j: (layer, 0, j)),
            pl.BlockSpec((None, d, tn), lambda i, j: (layer, 0, j)),
        ],
        out_specs=pl.BlockSpec((tm, tn), lambda i, j: (i, j)),
        out_shape=jax.ShapeDtypeStruct((t, f), BF16),
        scratch_shapes=[pltpu.VMEM((tm, d), BF16)],
        compiler_params=_params(("parallel", "arbitrary")),
        name="norm_gate_up",
    )(x, gain.reshape(1, d), wg, wu)


def _norm_matmul_kernel(x_ref, g_ref, w_ref, o_ref, h_ref):
    _norm_rows(x_ref, g_ref, h_ref)
    o_ref[...] = jnp.dot(h_ref[...], w_ref[...], preferred_element_type=F32).astype(o_ref.dtype)


def norm_matmul(x, gain, w, layer, tn_target, out_dtype):
    t, d = x.shape
    n = w.shape[-1]
    tm = _pick_tile(t, 640, 16)
    tn = _pick_tile(n, tn_target, 128)
    return pl.pallas_call(
        _norm_matmul_kernel,
        grid=(t // tm, n // tn),
        in_specs=[
            pl.BlockSpec((tm, d), lambda i, j: (i, 0)),
            pl.BlockSpec((1, d), lambda i, j: (0, 0)),
            pl.BlockSpec((None, d, tn), lambda i, j: (layer, 0, j)),
        ],
        out_specs=pl.BlockSpec((tm, tn), lambda i, j: (i, j)),
        out_shape=jax.ShapeDtypeStruct((t, n), out_dtype),
        scratch_shapes=[pltpu.VMEM((tm, d), BF16)],
        compiler_params=_params(("parallel", "arbitrary")),
        name="norm_matmul",
    )(x, gain.reshape(1, d), w)


def _residual_matmul_kernel(x_ref, a_ref, w_ref, o_ref, *, scale):
    o_ref[...] = x_ref[...] + scale * jnp.dot(a_ref[...], w_ref[...], preferred_element_type=F32)


def residual_matmul(x, a, w, layer, scale):
    t, d = x.shape
    k = a.shape[-1]
    tm = _pick_tile(t, 640, 16)
    tn = _pick_tile(d, 1024, 128)
    return pl.pallas_call(
        functools.partial(_residual_matmul_kernel, scale=scale),
        grid=(d // tn, t // tm),
        in_specs=[
            pl.BlockSpec((tm, tn), lambda j, i: (i, j)),
            pl.BlockSpec((tm, k), lambda j, i: (i, 0)),
            pl.BlockSpec((None, k, tn), lambda j, i: (layer, 0, j)),
        ],
        out_specs=pl.BlockSpec((tm, tn), lambda j, i: (i, j)),
        out_shape=jax.ShapeDtypeStruct((t, d), F32),
        compiler_params=_params(("parallel", "parallel")),
        name="residual_matmul",
    )(x, a, w)


def final_rmsnorm(x, gain, grp):
    d = x.shape[-1]
    nblk = grp.lp // BLK
    base = grp.base // BLK
    rows = grp.n_tok // BLK
    return pl.pallas_call(
        _rmsnorm_kernel,
        grid=(grp.bsz, rows),
        in_specs=[
            pl.BlockSpec((BLK, d), lambda b, i: (base + b * nblk + 1 + i, 0)),
            pl.BlockSpec((1, d), lambda b, i: (0, 0)),
        ],
        out_specs=pl.BlockSpec((None, BLK, d), lambda b, i: (b, i, 0)),
        out_shape=jax.ShapeDtypeStruct((grp.bsz, grp.n_tok, d), F32),
        compiler_params=_params(("parallel", "parallel")),
        name="final_norm",
    )(x, gain.reshape(1, d))


class Group:
    def __init__(self, bsz, n_tok, base):
        assert n_tok % GRID_W == 0 and n_tok // GRID_W >= WIN_ROWS
        self.bsz = bsz
        self.n_tok = n_tok
        self.lp = -(-(BLK + n_tok) // SEQ_ALIGN) * SEQ_ALIGN
        self.base = base
        self.rows = n_tok // GRID_W
        self.valid_end = BLK + n_tok

    @property
    def total(self):
        return self.bsz * self.lp


def _position_tables(grp):
    p = jnp.arange(grp.lp, dtype=jnp.int32)
    t = jnp.clip(p - BLK, 0, grp.n_tok - 1)
    is_tok = (p >= BLK) & (p < grp.valid_end)
    row = jnp.where(is_tok, t // GRID_W, 0).astype(F32)
    col = jnp.where(is_tok, t % GRID_W, 0).astype(F32)
    half = HEAD_DIM // 4
    freqs = ROPE_THETA ** (-jnp.arange(half, dtype=F32) / half)
    ang_r = row[:, None] * freqs[None, :]
    ang_c = col[:, None] * freqs[None, :]
    cos = jnp.concatenate([jnp.cos(ang_r), jnp.cos(ang_r), jnp.cos(ang_c), jnp.cos(ang_c)], axis=-1)
    sin = jnp.concatenate([-jnp.sin(ang_r), jnp.sin(ang_r), -jnp.sin(ang_c), jnp.sin(ang_c)], axis=-1)
    pk = jnp.arange(-(-grp.lp // FLASH_CK) * FLASH_CK, dtype=jnp.int32)
    key_bias = jnp.where((pk >= HEAD_PAD) & (pk < grp.valid_end), 0.0, NEG).astype(F32)
    return cos, sin, key_bias


def _swap_halves(x):
    lane = lax.broadcasted_iota(jnp.int32, x.shape, 1)
    return jnp.where(lane % 64 < 32, pltpu.roll(x, 96, 1), pltpu.roll(x, 32, 1))


def _prep_ac_kernel(a_ref, cq_ref, ck_ref, cv_ref, cos_ref, sin_ref, qg_ref, kg_ref,
                    qa_ref, ka_ref, va_ref, qc_ref, kvc_ref):
    cos = cos_ref[...]
    sin = sin_ref[...]
    scale = HEAD_DIM ** -0.5

    def norm_rope(x, gain):
        ms = jnp.mean(x * x, axis=-1, keepdims=True)
        y = x * lax.rsqrt(ms + EPS) * gain
        return y * cos + _swap_halves(y) * sin

    for h in range(A_HEADS):
        x = a_ref[:, COL_AQ + h * HEAD_DIM:COL_AQ + (h + 1) * HEAD_DIM]
        qa_ref[:, h * HEAD_DIM:(h + 1) * HEAD_DIM] = (norm_rope(x, qg_ref[...]) * (scale * LOG2E)).astype(BF16)
    for h in range(A_KV_HEADS):
        x = a_ref[:, COL_AK + h * HEAD_DIM:COL_AK + (h + 1) * HEAD_DIM]
        ka_ref[:, h * HEAD_DIM:(h + 1) * HEAD_DIM] = norm_rope(x, kg_ref[...]).astype(BF16)
    va_ref[...] = a_ref[:, COL_AV:COL_AV + A_KV_W].astype(BF16)
    qc_ref[...] = (cq_ref[...] * scale).astype(BF16)
    kvc_ref[:, :C_W] = ck_ref[...].astype(BF16)
    kvc_ref[:, C_W:] = cv_ref[...].astype(BF16)


def prep_ac(proj, grp, cos, sin, q_gain, k_gain):
    tr = 128
    nb = grp.lp // tr
    base = grp.base // tr
    a_w = A_Q_W + 2 * A_KV_W

    def rowmap(col):
        return lambda b, i: (base + b * nb + i, col)

    out_map = lambda b, i: (b * nb + i, 0)
    n = grp.total
    return pl.pallas_call(
        _prep_ac_kernel,
        grid=(grp.bsz, nb),
        in_specs=[
            pl.BlockSpec((tr, a_w), rowmap(0)),
            pl.BlockSpec((tr, C_W), rowmap(COL_CQ // C_W)),
            pl.BlockSpec((tr, C_W), rowmap(COL_CK // C_W)),
            pl.BlockSpec((tr, C_W), rowmap(COL_CV // C_W)),
            pl.BlockSpec((tr, HEAD_DIM), lambda b, i: (i, 0)),
            pl.BlockSpec((tr, HEAD_DIM), lambda b, i: (i, 0)),
            pl.BlockSpec((1, HEAD_DIM), lambda b, i: (0, 0)),
            pl.BlockSpec((1, HEAD_DIM), lambda b, i: (0, 0)),
        ],
        out_specs=[
            pl.BlockSpec((tr, A_Q_W), out_map),
            pl.BlockSpec((tr, A_KV_W), out_map),
            pl.BlockSpec((tr, A_KV_W), out_map),
            pl.BlockSpec((tr, C_W), out_map),
            pl.BlockSpec((tr, 2 * C_W), out_map),
        ],
        out_shape=[
            jax.ShapeDtypeStruct((n, A_Q_W), BF16),
            jax.ShapeDtypeStruct((n, A_KV_W), BF16),
            jax.ShapeDtypeStruct((n, A_KV_W), BF16),
            jax.ShapeDtypeStruct((n, C_W), BF16),
            jax.ShapeDtypeStruct((n, 2 * C_W), BF16),
        ],
        compiler_params=_params(("parallel", "parallel")),
        name="prep_ac",
    )(proj, proj, proj, proj, cos, sin, q_gain.reshape(1, HEAD_DIM), k_gain.reshape(1, HEAD_DIM))


FLASH_CK = 256
FLASH_UNROLL = 4


def _flash_kernel(q_ref, k_ref, v_ref, bias_ref, mix_ref, o_ref, q2_ref, s_ref, m_ref, l_ref, acc_ref, *, tq, nchunks):
    q2_ref[:tq] = q_ref[:, :HEAD_DIM]
    q2_ref[tq:] = q_ref[:, HEAD_DIM:]
    q2 = q2_ref[...]

    def scores(t, slot):
        start = pl.multiple_of(t * FLASH_CK, FLASH_CK)
        k = k_ref[pl.ds(start, FLASH_CK), :]
        s_ref[slot] = lax.dot_general(q2, k, (((1,), (1,)), ((), ())), preferred_element_type=F32)

    def step(t, slot, masked, prefetch):
        if prefetch:
            scores(t + 1, 1 - slot)

        def read_scores():
            s = s_ref[slot]
            return s + bias_ref[pl.ds(t, 1), :] if masked else s

        m = m_ref[...]
        m_new = jnp.maximum(m, jnp.max(read_scores(), axis=-1, keepdims=True))
        m_ref[...] = m_new
        p = jnp.exp2(read_scores() - jnp.concatenate([m_new, m_new], axis=-1))
        alpha = jnp.exp2(m - m_new)
        l_ref[...] = alpha * l_ref[...] + jnp.sum(p, axis=-1, keepdims=True)
        v = v_ref[pl.ds(pl.multiple_of(t * FLASH_CK, FLASH_CK), FLASH_CK), :]
        acc_ref[...] = alpha * acc_ref[...] + jnp.dot(p.astype(BF16), v, preferred_element_type=F32)

    m_ref[...] = jnp.full(m_ref.shape, -jnp.inf, F32)
    l_ref[...] = jnp.zeros(l_ref.shape, F32)
    acc_ref[...] = jnp.zeros(acc_ref.shape, F32)
    scores(0, 0)
    step(0, 0, True, nchunks > 1)
    if nchunks > 1:
        t0 = 1
        for _ in range((nchunks - 2) % FLASH_UNROLL):
            step(t0, t0 % 2, False, True)
            t0 += 1

        def group(j, carry):
            for u in range(FLASH_UNROLL):
                step(t0 + FLASH_UNROLL * j + u, (t0 + u) % 2, False, True)
            return carry

        lax.fori_loop(0, (nchunks - 1 - t0) // FLASH_UNROLL, group, 0)
        step(nchunks - 1, (nchunks - 1) % 2, True, False)
    o = acc_ref[...] / l_ref[...]
    o_ref[:, :HEAD_DIM] = o[:tq].astype(o_ref.dtype)
    o_ref[:, HEAD_DIM:] = o[tq:].astype(o_ref.dtype)


def flash_attention(qa, ka, va, key_bias, mix, grp):
    lp = grp.lp
    lkv = ka.shape[0] // grp.bsz
    assert lkv % FLASH_CK == 0 and lkv - grp.valid_end <= FLASH_CK and HEAD_PAD <= FLASH_CK
    tq = _pick_tile(lp, 384, 128)
    nchunks = lkv // FLASH_CK
    nq = lp // tq
    gw = 2 * HEAD_DIM
    return pl.pallas_call(
        functools.partial(_flash_kernel, tq=tq, nchunks=nchunks),
        grid=(grp.bsz, A_KV_HEADS, nq),
        in_specs=[
            pl.BlockSpec((tq, gw), lambda b, g, i: (b * nq + i, g)),
            pl.BlockSpec((lkv, HEAD_DIM), lambda b, g, i: (b, g)),
            pl.BlockSpec((lkv, HEAD_DIM), lambda b, g, i: (b, g)),
            pl.BlockSpec((nchunks, FLASH_CK), lambda b, g, i: (0, 0)),
            pl.BlockSpec(memory_space=pl.ANY),
        ],
        out_specs=pl.BlockSpec((tq, gw), lambda b, g, i: (grp.base // tq + b * nq + i, MIX_COL_A // gw + g)),
        out_shape=jax.ShapeDtypeStruct(mix.shape, mix.dtype),
        input_output_aliases={4: 0},
        scratch_shapes=[
            pltpu.VMEM((2 * tq, HEAD_DIM), BF16),
            pltpu.VMEM((2, 2 * tq, FLASH_CK), F32),
            pltpu.VMEM((2 * tq, HEAD_DIM), F32),
            pltpu.VMEM((2 * tq, HEAD_DIM), F32),
            pltpu.VMEM((2 * tq, HEAD_DIM), F32),
        ],
        compiler_params=_params(("parallel", "parallel", "parallel")),
        name="flash_attention",
    )(qa, ka, va, key_bias.reshape(nchunks, FLASH_CK), mix)


def pad_keys(x, grp, value=0):
    lkv = -(-grp.lp // FLASH_CK) * FLASH_CK
    x = x.reshape(grp.bsz, grp.lp, -1)
    x = jnp.pad(x, ((0, 0), (0, lkv - grp.lp), (0, 0)), constant_values=value)
    return x.reshape(grp.bsz * lkv, -1)


NA_KEYS = BLK + WIN_ROWS * GRID_W
NA_META_VARIANT = WIN_ROWS


def _na_bias_table(rel_bias, meta_bias):
    rel_bias = rel_bias.astype(F32)
    qc = jnp.arange(GRID_W, dtype=jnp.int32)
    c = jnp.arange(GRID_W, dtype=jnp.int32)
    col_start = jnp.clip(qc - WIN_COLS // 2, 0, GRID_W - WIN_COLS)
    in_win = (c[None, :] >= col_start[:, None]) & (c[None, :] < col_start[:, None] + WIN_COLS)
    dc = jnp.clip(c[None, :] - qc[:, None] + (WIN_COLS - 1), 0, 2 * WIN_COLS - 2)
    full = jnp.where(in_win[None, None], rel_bias[:, :, dc], NEG)
    r = jnp.arange(WIN_ROWS, dtype=jnp.int32)
    variants = []
    for d0 in range(WIN_ROWS):
        band = full[:, d0 + r]
        variants.append(jnp.moveaxis(band, 1, 2).reshape(C_HEADS, GRID_W, WIN_ROWS * GRID_W))
    meta_band = jnp.broadcast_to(variants[WIN_ROWS - 1][:, :1], (C_HEADS, GRID_W, WIN_ROWS * GRID_W))
    variants.append(meta_band)
    band = jnp.stack(variants)
    mcol = jnp.concatenate([jnp.full((C_HEADS, HEAD_PAD), NEG, F32), meta_bias.astype(F32)], axis=-1)
    mcol = jnp.broadcast_to(mcol[None, :, None, :], (WIN_ROWS + 1, C_HEADS, GRID_W, BLK))
    return jnp.concatenate([mcol, band], axis=-1)


def _na_row_start(j, rows):
    return jnp.clip(j - 1 - WIN_ROWS // 2, 0, rows - WIN_ROWS)


NA_PAIR = 2


def _na_kernel(q_ref, meta_ref, *rest, rows):
    band_refs = rest[:NA_PAIR * WIN_ROWS]
    bias_ref, _, o_ref = rest[NA_PAIR * WIN_ROWS:]
    meta = meta_ref[...]
    items = []
    for u in range(NA_PAIR):
        j = jnp.minimum(pl.program_id(1) * NA_PAIR + u, rows)
        variant = jnp.where(j == 0, NA_META_VARIANT, _na_row_start(j, rows) - (j - 1) + (WIN_ROWS - 1))
        kv = jnp.concatenate([meta] + [r[...] for r in band_refs[u * WIN_ROWS:(u + 1) * WIN_ROWS]], axis=0)
        for h in range(C_HEADS):
            items.append((slice(u * BLK, (u + 1) * BLK), slice(h * HEAD_DIM, (h + 1) * HEAD_DIM), kv, variant, h))
    s = [lax.dot_general(q_ref[rs, sl], kv[:, sl], (((1,), (1,)), ((), ())), preferred_element_type=F32)
         + bias_ref[variant, h] for rs, sl, kv, variant, h in items]
    e = [jnp.exp(x - jnp.max(x, axis=-1, keepdims=True)) for x in s]
    p = [(x / jnp.sum(x, axis=-1, keepdims=True)).astype(BF16) for x in e]
    o = [jnp.dot(p[n], kv[:, C_W + sl.start:C_W + sl.stop], preferred_element_type=F32)
         for n, (rs, sl, kv, variant, h) in enumerate(items)]
    for n, (rs, sl, kv, variant, h) in enumerate(items):
        o_ref[rs, sl] = o[n].astype(o_ref.dtype)


def neighbourhood_attention(qc, kvc, bias_table, mix, grp):
    nblk = grp.lp // BLK
    rows = grp.rows
    assert nblk % NA_PAIR == 0
    npair = nblk // NA_PAIR
    qrows = NA_PAIR * BLK

    def band_map(u, r):
        def index_map(b, j):
            jj = jnp.minimum(j * NA_PAIR + u, rows)
            return (b * nblk + 1 + _na_row_start(jj, rows) + r, 0)
        return index_map

    in_specs = [
        pl.BlockSpec((qrows, C_W), lambda b, j: (b * npair + j, 0)),
        pl.BlockSpec((BLK, 2 * C_W), lambda b, j: (b * nblk, 0)),
    ]
    in_specs += [pl.BlockSpec((BLK, 2 * C_W), band_map(u, r)) for u in range(NA_PAIR) for r in range(WIN_ROWS)]
    in_specs += [pl.BlockSpec(bias_table.shape, lambda b, j: (0, 0, 0, 0)), pl.BlockSpec(memory_space=pl.ANY)]
    return pl.pallas_call(
        functools.partial(_na_kernel, rows=rows),
        grid=(grp.bsz, npair),
        in_specs=in_specs,
        out_specs=pl.BlockSpec((qrows, C_W), lambda b, j: (grp.base // qrows + b * npair + j, MIX_COL_C // C_W)),
        out_shape=jax.ShapeDtypeStruct(mix.shape, mix.dtype),
        input_output_aliases={len(in_specs) - 1: 0},
        compiler_params=_params(("parallel", "parallel")),
        name="neighbourhood_attention",
    )(qc, kvc, *([kvc] * (NA_PAIR * WIN_ROWS)), bias_table, mix)


HALO = 8
GATE_G = 0
GATE_BETA = 2 * B_HEADS


def _dn_prep_kernel(xp_ref, xc_ref, xn_ref, gate_ref, w_ref, a_ref, dtb_ref, qkv_ref, gb_ref, win_ref,
                    *, tb, valid_end):
    i = pl.program_id(1)
    part = pl.program_id(2)
    pos_w = i * tb - HALO + lax.broadcasted_iota(jnp.int32, (tb + 2 * HALO, 1), 0)
    ok_w = (pos_w >= HEAD_PAD) & (pos_w < valid_end)
    win_ref[0:HALO] = xp_ref[...]
    win_ref[HALO:HALO + tb] = xc_ref[...]
    win_ref[HALO + tb:] = xn_ref[...]
    win_ref[...] = jnp.where(ok_w, win_ref[...], 0.0)
    y = jnp.zeros((tb, B_W), F32)
    for j in range(CONV_K):
        y = y + win_ref[pl.ds(HALO - CONV_K // 2 + j, tb), :] * w_ref[j:j + 1, :]
    y = y * jax.nn.sigmoid(y)
    pos = i * tb + lax.broadcasted_iota(jnp.int32, (tb, 1), 0)
    ok = (pos >= HEAD_PAD) & (pos < valid_end)
    unit = jnp.where(part == 0, HEAD_DIM ** -0.5, 1.0)
    for h in range(B_HEADS):
        yh = y[:, h * HEAD_DIM:(h + 1) * HEAD_DIM]
        inv = lax.rsqrt(jnp.sum(yh * yh, axis=-1, keepdims=True) + EPS) * unit
        fac = jnp.where(part == 2, 1.0, inv)
        qkv_ref[:, h * HEAD_DIM:(h + 1) * HEAD_DIM] = jnp.where(ok, yh * fac, 0.0)

    @pl.when(part == 0)
    def _():
        x = gate_ref[...]
        lane = lax.broadcasted_iota(jnp.int32, x.shape, 1)
        z = x + dtb_ref[...]
        softplus = jnp.maximum(z, 0.0) + jnp.log1p(jnp.exp(-jnp.abs(z)))
        g = -jnp.exp(a_ref[...]) * softplus
        beta = jax.nn.sigmoid(x)
        out = jnp.where(lane < GATE_BETA, g, jnp.where(lane < 2 * GATE_BETA, beta, 0.0))
        gb_ref[...] = jnp.where(ok, out, 0.0)


def dn_prep(proj, grp, conv_w, a_log, dt_bias):
    tb = 128
    nb = grp.lp // tb
    base = grp.base // tb
    hb = tb // HALO
    last_halo = proj.shape[0] // HALO - 1
    qkv_col = COL_BQKV // B_W
    pad = jnp.zeros((1, 128 - 2 * B_HEADS), F32)
    a_row = jnp.concatenate([a_log.astype(F32).reshape(1, 2 * B_HEADS), pad], axis=-1)
    dtb_row = jnp.concatenate([dt_bias.astype(F32).reshape(1, 2 * B_HEADS), pad], axis=-1)
    return pl.pallas_call(
        functools.partial(_dn_prep_kernel, tb=tb, valid_end=grp.valid_end),
        grid=(grp.bsz, nb, 3),
        in_specs=[
            pl.BlockSpec((HALO, B_W), lambda b, i, p: (jnp.maximum((base + b * nb + i) * hb - 1, 0), qkv_col + p)),
            pl.BlockSpec((tb, B_W), lambda b, i, p: (base + b * nb + i, qkv_col + p)),
            pl.BlockSpec((HALO, B_W), lambda b, i, p: (jnp.minimum((base + b * nb + i + 1) * hb, last_halo), qkv_col + p)),
            pl.BlockSpec((tb, 128), lambda b, i, p: (base + b * nb + i, COL_GATE // 128)),
            pl.BlockSpec((CONV_K, B_W), lambda b, i, p: (0, p)),
            pl.BlockSpec((1, 128), lambda b, i, p: (0, 0)),
            pl.BlockSpec((1, 128), lambda b, i, p: (0, 0)),
        ],
        out_specs=[
            pl.BlockSpec((tb, B_W), lambda b, i, p: (b * nb + i, p)),
            pl.BlockSpec((tb, 128), lambda b, i, p: (b * nb + i, 0)),
        ],
        out_shape=[
            jax.ShapeDtypeStruct((grp.total, 3 * B_W), F32),
            jax.ShapeDtypeStruct((grp.total, 128), F32),
        ],
        scratch_shapes=[pltpu.VMEM((tb + 2 * HALO, B_W), F32)],
        compiler_params=_params(("parallel", "parallel", "arbitrary")),
        name="dn_prep",
    )(proj, proj, proj, proj, conv_w, a_row, dtb_row)


def _nt(a, b):
    return lax.dot_general(a, b, (((1,), (1,)), ((), ())), preferred_element_type=F32)


def _dn_scan_kernel(qf_ref, kf_ref, vf_ref, gf_ref, qb_ref, kb_ref, vb_ref, gb_ref, of_ref, ob_ref, s_ref):
    @pl.when(pl.program_id(1) == 0)
    def _():
        s_ref[...] = jnp.zeros_like(s_ref)

    ri = lax.broadcasted_iota(jnp.int32, (CHUNK, CHUNK), 0)
    ci = lax.broadcasted_iota(jnp.int32, (CHUNK, CHUNK), 1)
    eye = (ri == ci).astype(F32)
    eye128 = (lax.broadcasted_iota(jnp.int32, (128, 128), 0) == lax.broadcasted_iota(jnp.int32, (128, 128), 1)).astype(F32)
    dirs = ((qf_ref, kf_ref, vf_ref, gf_ref, of_ref), (qb_ref, kb_ref, vb_ref, gb_ref, ob_ref))
    n = 2 * B_HEADS
    incl, strict, gcc, gcr, g_tot, beta, refs = [], [], [], [], [], [], []
    for d, (q_ref, k_ref, v_ref, g_ref, o_ref) in enumerate(dirs):
        incl_d = (ri >= ci) if d == 0 else (ri <= ci)
        strict_d = (ri > ci) if d == 0 else (ri < ci)
        gates = g_ref[...]
        cum = jnp.dot(incl_d.astype(F32), gates, preferred_element_type=F32, precision=lax.Precision.HIGHEST)
        cum_t = lax.dot_general(eye128, cum, (((1,), (1,)), ((), ())), preferred_element_type=F32,
                                precision=lax.Precision.HIGHEST)
        last = CHUNK - 1 if d == 0 else 0
        for h in range(B_HEADS):
            c = GATE_G + d * B_HEADS + h
            sl = slice(h * HEAD_DIM, (h + 1) * HEAD_DIM)
            incl.append(incl_d)
            strict.append(strict_d)
            gcc.append(cum[:, c:c + 1])
            gcr.append(cum_t[c:c + 1, :])
            g_tot.append(cum_t[c:c + 1, last:last + 1])
            beta.append(gates[:, GATE_BETA + c:GATE_BETA + c + 1])
            refs.append((q_ref, k_ref, v_ref, o_ref, sl))

    decay = [jnp.exp(jnp.where(incl[i], gcc[i] - gcr[i], -jnp.inf)) for i in range(n)]
    k = [refs[i][1][:, refs[i][4]] for i in range(n)]
    kb = [k[i] * beta[i] for i in range(n)]
    k16 = [k[i].astype(BF16) for i in range(n)]
    x = [-jnp.where(strict[i], _nt(kb[i].astype(BF16), k16[i]) * decay[i], 0.0) for i in range(n)]
    t = [x[i] + eye for i in range(n)]
    p = x
    for _ in range(5):
        p16 = [p[i].astype(BF16) for i in range(n)]
        p = [jnp.dot(p16[i], p16[i], preferred_element_type=F32) for i in range(n)]
        t = [t[i] + jnp.dot(t[i].astype(BF16), p[i].astype(BF16), preferred_element_type=F32) for i in range(n)]
    egc = [jnp.exp(gcc[i]) for i in range(n)]
    rhs = [jnp.concatenate([kb[i] * egc[i], refs[i][2][:, refs[i][4]] * beta[i]], axis=-1).astype(BF16) for i in range(n)]
    wu = [jnp.dot(t[i].astype(BF16), rhs[i], preferred_element_type=F32) for i in range(n)]
    q = [refs[i][0][:, refs[i][4]] for i in range(n)]
    qk = [jnp.where(incl[i], _nt(q[i].astype(BF16), k16[i]) * decay[i], 0.0).astype(BF16) for i in range(n)]
    s = [s_ref[i] for i in range(n)]
    lhs = [jnp.concatenate([wu[i][:, :HEAD_DIM], q[i] * egc[i]], axis=0).astype(BF16) for i in range(n)]
    both = [jnp.dot(lhs[i], s[i].astype(BF16), preferred_element_type=F32) for i in range(n)]
    v16 = [(wu[i][:, HEAD_DIM:] - both[i][:CHUNK]).astype(BF16) for i in range(n)]
    for i in range(n):
        o_ref, sl = refs[i][3], refs[i][4]
        o_ref[:, sl] = both[i][CHUNK:] + jnp.dot(qk[i], v16[i], preferred_element_type=F32)
    kd = [(k[i] * jnp.exp(g_tot[i] - gcc[i])).astype(BF16) for i in range(n)]
    for i in range(n):
        s_ref[i] = s[i] * jnp.exp(g_tot[i]) + lax.dot_general(
            kd[i], v16[i], (((0,), (0,)), ((), ())), preferred_element_type=F32)


def dn_scan(qkv, gb, grp):
    nc = grp.lp // BLK

    def fw(col):
        return lambda b, c: (b * nc + c, col)

    def bw(col):
        return lambda b, c: (b * nc + nc - 1 - c, col)

    blk = (BLK, B_W)
    return pl.pallas_call(
        _dn_scan_kernel,
        grid=(grp.bsz, nc),
        in_specs=[
            pl.BlockSpec(blk, fw(0)), pl.BlockSpec(blk, fw(1)), pl.BlockSpec(blk, fw(2)), pl.BlockSpec((BLK, 128), fw(0)),
            pl.BlockSpec(blk, bw(0)), pl.BlockSpec(blk, bw(1)), pl.BlockSpec(blk, bw(2)), pl.BlockSpec((BLK, 128), bw(0)),
        ],
        out_specs=[pl.BlockSpec(blk, fw(0)), pl.BlockSpec(blk, bw(0))],
        out_shape=[jax.ShapeDtypeStruct((grp.total, B_W), F32)] * 2,
        scratch_shapes=[pltpu.VMEM((2 * B_HEADS, HEAD_DIM, HEAD_DIM), F32)],
        compiler_params=_params(("parallel", "arbitrary")),
        name="dn_scan",
    )(qkv, qkv, qkv, gb, qkv, qkv, qkv, gb)


def _dn_post_kernel(of_ref, ob_ref, z_ref, gain_ref, mix_ref, o_ref):
    for h in range(B_HEADS):
        lo, hi = h * HEAD_DIM, (h + 1) * HEAD_DIM
        o = of_ref[:, lo:hi] + ob_ref[:, lo:hi]
        ms = jnp.mean(o * o, axis=-1, keepdims=True)
        z = z_ref[:, lo:hi]
        o_ref[:, lo:hi] = (o * lax.rsqrt(ms + EPS) * gain_ref[...] * (z * jax.nn.sigmoid(z))).astype(o_ref.dtype)


def dn_post(o_fw, o_bw, proj, mix, grp, out_gain):
    tb = 128
    nb = grp.lp // tb
    base = grp.base // tb
    return pl.pallas_call(
        _dn_post_kernel,
        grid=(grp.bsz * nb,),
        in_specs=[
            pl.BlockSpec((tb, B_W), lambda i: (i, 0)),
            pl.BlockSpec((tb, B_W), lambda i: (i, 0)),
            pl.BlockSpec((tb, B_W), lambda i: (base + i, COL_BZ // B_W)),
            pl.BlockSpec((1, HEAD_DIM), lambda i: (0, 0)),
            pl.BlockSpec(memory_space=pl.ANY),
        ],
        out_specs=pl.BlockSpec((tb, B_W), lambda i: (base + i, MIX_COL_B // B_W)),
        out_shape=jax.ShapeDtypeStruct(mix.shape, mix.dtype),
        input_output_aliases={4: 0},
        compiler_params=_params(("parallel",)),
        name="dn_post",
    )(o_fw, o_bw, proj, out_gain.reshape(1, HEAD_DIM), mix)


def _permute_w_in(w_in):
    sizes = (A_Q_W, A_KV_W, A_KV_W, 3 * B_W, B_W, 2 * B_HEADS, 2 * B_HEADS, C_W, C_W, C_W)
    offs = [0]
    for s in sizes:
        offs.append(offs[-1] + s)
    seg = [w_in[..., offs[n]:offs[n + 1]] for n in range(len(sizes))]
    aq, ak, av, bqkv, bz, ba, bb, cq, ck, cv = seg
    used = COL_GATE + 4 * B_HEADS
    pad = jnp.zeros(w_in.shape[:-1] + (IN_WIDTH_PAD - used,), w_in.dtype)
    return jnp.concatenate([aq, ak, av, bqkv, bz, cq, ck, cv, ba, bb, pad], axis=-1).astype(BF16)


def _stack_tokens(xs, groups, meta_tokens):
    parts = []
    for x, grp in zip(xs, groups):
        d = x.shape[-1]
        meta = jnp.broadcast_to(meta_tokens.astype(x.dtype)[None], (grp.bsz, N_META, d))
        front = jnp.zeros((grp.bsz, HEAD_PAD, d), x.dtype)
        tail = jnp.zeros((grp.bsz, grp.lp - grp.valid_end, d), x.dtype)
        parts.append(jnp.concatenate([front, meta, x, tail], axis=1).reshape(grp.total, d))
    return jnp.concatenate(parts, axis=0)


def kernel(x_prompt, x_sample, meta_tokens, ffn1_norm, ffn1_w_gate, ffn1_w_up, ffn1_w_down, mix_norm, w_in,
           attn_q_norm, attn_k_norm, dn_conv_w, dn_a_log, dn_dt_bias, dn_out_norm, na_rel_bias, na_meta_bias,
           w_out, ffn2_norm, ffn2_w_gate, ffn2_w_up, ffn2_w_down, final_norm):
    depth = w_in.shape[0]
    g_prompt = Group(x_prompt.shape[0], x_prompt.shape[1], 0)
    g_sample = Group(x_sample.shape[0], x_sample.shape[1], g_prompt.total)
    groups = (g_prompt, g_sample)
    x = _stack_tokens((x_prompt, x_sample), groups, meta_tokens)
    tables = [_position_tables(grp) for grp in groups]

    w1g, w1u, w1d = (w.astype(BF16) for w in (ffn1_w_gate, ffn1_w_up, ffn1_w_down))
    w2g, w2u, w2d = (w.astype(BF16) for w in (ffn2_w_gate, ffn2_w_up, ffn2_w_down))
    w_in_p = _permute_w_in(w_in)
    w_out_b = jnp.concatenate([w_out[:, A_Q_W:A_Q_W + B_W], w_out[:, :A_Q_W], w_out[:, A_Q_W + B_W:]], axis=1).astype(BF16)
    mix = jnp.zeros((x.shape[0], MIX_WIDTH), BF16)

    for l in range(depth):
        x = residual_matmul(x, norm_gate_up(x, ffn1_norm[l], w1g, w1u, l), w1d, l, 0.5)

        proj = norm_matmul(x, mix_norm[l], w_in_p, l, IN_TILE, F32)
        bias_table = _na_bias_table(na_rel_bias[l], na_meta_bias[l])
        for grp, (cos, sin, key_bias) in zip(groups, tables):
            qa, ka, va, qc, kvc = prep_ac(proj, grp, cos, sin, attn_q_norm[l], attn_k_norm[l])
            mix = flash_attention(qa, pad_keys(ka, grp), pad_keys(va, grp), key_bias, mix, grp)
            mix = neighbourhood_attention(qc, kvc, bias_table, mix, grp)
            qkv, gb = dn_prep(proj, grp, dn_conv_w[l], dn_a_log[l], dn_dt_bias[l])
            o_fw, o_bw = dn_scan(qkv, gb, grp)
            mix = dn_post(o_fw, o_bw, proj, mix, grp, dn_out_norm[l])
        x = residual_matmul(x, mix, w_out_b, l, 1.0)

        x = residual_matmul(x, norm_gate_up(x, ffn2_norm[l], w2g, w2u, l), w2d, l, 0.5)

    return tuple(final_rmsnorm(x, final_norm, grp) for grp in groups)
```

```python
import functools

import jax
import jax.numpy as jnp
from jax import lax
from jax.experimental import pallas as pl
from jax.experimental.pallas import tpu as pltpu

F32 = jnp.float32
BF16 = jnp.bfloat16

HEAD_DIM = 128
N_META = 16
GRID_W = 64
EPS = 1e-6
A_HEADS = 4
A_KV_HEADS = 2
ROPE_THETA = 10000.0
B_HEADS = 8
CONV_K = 5
CHUNK = 64
C_HEADS = 4
WIN_ROWS = 8
WIN_COLS = 16

A_Q_W = A_HEADS * HEAD_DIM
A_KV_W = A_KV_HEADS * HEAD_DIM
B_W = B_HEADS * HEAD_DIM
C_W = C_HEADS * HEAD_DIM
MIX_WIDTH = A_Q_W + B_W + C_W

BLK = 64
HEAD_PAD = BLK - N_META
SEQ_ALIGN = 128
NEG = -1e30
LOG2E = 1.4426950408889634

COL_AQ = 0
COL_AK = COL_AQ + A_Q_W
COL_AV = COL_AK + A_KV_W
COL_BQKV = COL_AV + A_KV_W
COL_BZ = COL_BQKV + 3 * B_W
COL_CQ = COL_BZ + B_W
COL_CK = COL_CQ + C_W
COL_CV = COL_CK + C_W
COL_GATE = COL_CV + C_W
MIX_COL_B = 0
MIX_COL_A = MIX_COL_B + B_W
MIX_COL_C = MIX_COL_A + A_Q_W
IN_TILE = 768
IN_WIDTH_PAD = -(-(COL_GATE + 128) // IN_TILE) * IN_TILE

V7X_VMEM_BYTES = 64 * 1024 * 1024
VMEM_LIMIT = 52 * 1024 * 1024


def _params(sem):
    return pltpu.CompilerParams(dimension_semantics=sem, vmem_limit_bytes=VMEM_LIMIT)


def _pick_tile(n, target, mult):
    best = None
    for t in range(mult, min(n, target) + 1, mult):
        if n % t == 0:
            best = t
    assert best is not None, (n, target, mult)
    return best


def _rmsnorm_kernel(x_ref, g_ref, o_ref):
    x = x_ref[...]
    ms = jnp.mean(x * x, axis=-1, keepdims=True)
    o_ref[...] = (x * lax.rsqrt(ms + EPS) * g_ref[...]).astype(o_ref.dtype)


NORM_SUB_ROWS = 16
NORM_UNROLL = 4


def _norm_rows(x_ref, g_ref, h_ref):
    @pl.when(pl.program_id(1) == 0)
    def _():
        rows = x_ref.shape[0]
        sub = NORM_SUB_ROWS
        unroll = max(u for u in range(1, NORM_UNROLL + 1) if rows % (sub * u) == 0)

        def body(r, carry):
            for u in range(unroll):
                rs = pl.ds(pl.multiple_of((r * unroll + u) * sub, sub), sub)
                x = x_ref[rs, :]
                ms = jnp.mean(x * x, axis=-1, keepdims=True)
                h_ref[rs, :] = (x * lax.rsqrt(ms + EPS) * g_ref[...]).astype(h_ref.dtype)
            return carry

        lax.fori_loop(0, rows // (sub * unroll), body, 0)


def _norm_gate_up_kernel(x_ref, g_ref, wg_ref, wu_ref, o_ref, h_ref):
    _norm_rows(x_ref, g_ref, h_ref)
    h = h_ref[...]
    g = jnp.dot(h, wg_ref[...], preferred_element_type=F32)
    u = jnp.dot(h, wu_ref[...], preferred_element_type=F32)
    o_ref[...] = (g * jax.nn.sigmoid(g) * u).astype(o_ref.dtype)


def norm_gate_up(x, gain, wg, wu, layer):
    t, d = x.shape
    f = wg.shape[-1]
    tm = _pick_tile(t, 640, 16)
    tn = _pick_tile(f, 512, 128)
    return pl.pallas_call(
        _norm_gate_up_kernel,
        grid=(t // tm, f // tn),
        in_specs=[
            pl.BlockSpec((tm, d), lambda i, j: (i, 0)),
            pl.BlockSpec((1, d), lambda i, j: (0, 0)),
            pl.BlockSpec((None, d, tn), lambda i, j: (layer, 0, j)),
            pl.BlockSpec((None, d, tn), lambda i, j: (layer, 0, j)),
        ],
        out_specs=pl.BlockSpec((tm, tn), lambda i, j: (i, j)),
        out_shape=jax.ShapeDtypeStruct((t, f), BF16),
        scratch_shapes=[pltpu.VMEM((tm, d), BF16)],
        compiler_params=_params(("parallel", "arbitrary")),
        name="norm_gate_up",
    )(x, gain.reshape(1, d), wg, wu)


def _norm_matmul_kernel(x_ref, g_ref, w_ref, o_ref, h_ref):
    _norm_rows(x_ref, g_ref, h_ref)
    o_ref[...] = jnp.dot(h_ref[...], w_ref[...], preferred_element_type=F32).astype(o_ref.dtype)


def norm_matmul(x, gain, w, layer, tn_target, out_dtype):
    t, d = x.shape
    n = w.shape[-1]
    tm = _pick_tile(t, 640, 16)
    tn = _pick_tile(n, tn_target, 128)
    return pl.pallas_call(
        _norm_matmul_kernel,
        grid=(t // tm, n // tn),
        in_specs=[
            pl.BlockSpec((tm, d), lambda i, j: (i, 0)),
            pl.BlockSpec((1, d), lambda i, j: (0, 0)),
            pl.BlockSpec((None, d, tn), lambda i, j: (layer, 0, j)),
        ],
        out_specs=pl.BlockSpec((tm, tn), lambda i, j: (i, j)),
        out_shape=jax.ShapeDtypeStruct((t, n), out_dtype),
        scratch_shapes=[pltpu.VMEM((tm, d), BF16)],
        compiler_params=_params(("parallel", "arbitrary")),
        name="norm_matmul",
    )(x, gain.reshape(1, d), w)


def _residual_matmul_kernel(x_ref, a_ref, w_ref, o_ref, *, scale):
    o_ref[...] = x_ref[...] + scale * jnp.dot(a_ref[...], w_ref[...], preferred_element_type=F32)


def residual_matmul(x, a, w, layer, scale):
    t, d = x.shape
    k = a.shape[-1]
    tm = _pick_tile(t, 640, 16)
    tn = _pick_tile(d, 1024, 128)
    return pl.pallas_call(
        functools.partial(_residual_matmul_kernel, scale=scale),
        grid=(d // tn, t // tm),
        in_specs=[
            pl.BlockSpec((tm, tn), lambda j, i: (i, j)),
            pl.BlockSpec((tm, k), lambda j, i: (i, 0)),
            pl.BlockSpec((None, k, tn), lambda j, i: (layer, 0, j)),
        ],
        out_specs=pl.BlockSpec((tm, tn), lambda j, i: (i, j)),
        out_shape=jax.ShapeDtypeStruct((t, d), F32),
        compiler_params=_params(("parallel", "parallel")),
        name="residual_matmul",
    )(x, a, w)


def final_rmsnorm(x, gain, grp):
    d = x.shape[-1]
    nblk = grp.lp // BLK
    base = grp.base // BLK
    rows = grp.n_tok // BLK
    return pl.pallas_call(
        _rmsnorm_kernel,
        grid=(grp.bsz, rows),
        in_specs=[
            pl.BlockSpec((BLK, d), lambda b, i: (base + b * nblk + 1 + i, 0)),
            pl.BlockSpec((1, d), lambda b, i: (0, 0)),
        ],
        out_specs=pl.BlockSpec((None, BLK, d), lambda b, i: (b, i, 0)),
        out_shape=jax.ShapeDtypeStruct((grp.bsz, grp.n_tok, d), F32),
        compiler_params=_params(("parallel", "parallel")),
        name="final_norm",
    )(x, gain.reshape(1, d))


class Group:
    def __init__(self, bsz, n_tok, base):
        assert n_tok % GRID_W == 0 and n_tok // GRID_W >= WIN_ROWS
        self.bsz = bsz
        self.n_tok = n_tok
        self.lp = -(-(BLK + n_tok) // SEQ_ALIGN) * SEQ_ALIGN
        self.base = base
        self.rows = n_tok // GRID_W
        self.valid_end = BLK + n_tok

    @property
    def total(self):
        return self.bsz * self.lp

    def row_tile(self, target):
        best = SEQ_ALIGN
        for t in range(SEQ_ALIGN, target + 1, SEQ_ALIGN):
            if self.lp % t == 0 and self.base % t == 0:
                best = t
        return best


def _position_tables(grp):
    p = jnp.arange(grp.lp, dtype=jnp.int32)
    t = jnp.clip(p - BLK, 0, grp.n_tok - 1)
    is_tok = (p >= BLK) & (p < grp.valid_end)
    row = jnp.where(is_tok, t // GRID_W, 0).astype(F32)
    col = jnp.where(is_tok, t % GRID_W, 0).astype(F32)
    half = HEAD_DIM // 4
    freqs = ROPE_THETA ** (-jnp.arange(half, dtype=F32) / half)
    ang_r = row[:, None] * freqs[None, :]
    ang_c = col[:, None] * freqs[None, :]
    cos = jnp.concatenate([jnp.cos(ang_r), jnp.cos(ang_r), jnp.cos(ang_c), jnp.cos(ang_c)], axis=-1)
    sin = jnp.concatenate([-jnp.sin(ang_r), jnp.sin(ang_r), -jnp.sin(ang_c), jnp.sin(ang_c)], axis=-1)
    pk = jnp.arange(-(-grp.lp // FLASH_CK) * FLASH_CK, dtype=jnp.int32)
    key_bias = jnp.where((pk >= HEAD_PAD) & (pk < grp.valid_end), 0.0, NEG).astype(F32)
    return cos, sin, key_bias


def _swap_halves(x):
    lane = lax.broadcasted_iota(jnp.int32, x.shape, 1)
    return jnp.where(lane % 64 < 32, pltpu.roll(x, 96, 1), pltpu.roll(x, 32, 1))


def _prep_ac_kernel(a_ref, cq_ref, ck_ref, cv_ref, cos_ref, sin_ref, qg_ref, kg_ref,
                    qa_ref, ka_ref, va_ref, qc_ref, kvc_ref):
    cos = cos_ref[...]
    sin = sin_ref[...]
    scale = HEAD_DIM ** -0.5

    def norm_rope(x, gain):
        ms = jnp.mean(x * x, axis=-1, keepdims=True)
        y = x * lax.rsqrt(ms + EPS) * gain
        return y * cos + _swap_halves(y) * sin

    for h in range(A_HEADS):
        x = a_ref[:, COL_AQ + h * HEAD_DIM:COL_AQ + (h + 1) * HEAD_DIM]
        qa_ref[:, h * HEAD_DIM:(h + 1) * HEAD_DIM] = (norm_rope(x, qg_ref[...]) * (scale * LOG2E)).astype(BF16)
    for h in range(A_KV_HEADS):
        x = a_ref[:, COL_AK + h * HEAD_DIM:COL_AK + (h + 1) * HEAD_DIM]
        ka_ref[:, h * HEAD_DIM:(h + 1) * HEAD_DIM] = norm_rope(x, kg_ref[...]).astype(BF16)
    va_ref[...] = a_ref[:, COL_AV:COL_AV + A_KV_W].astype(BF16)
    qc_ref[...] = (cq_ref[...] * scale).astype(BF16)
    kvc_ref[:, :C_W] = ck_ref[...].astype(BF16)
    kvc_ref[:, C_W:] = cv_ref[...].astype(BF16)


def prep_ac(proj, grp, cos, sin, q_gain, k_gain):
    tr = grp.row_tile(384)
    nb = grp.lp // tr
    base = grp.base // tr
    a_w = A_Q_W + 2 * A_KV_W

    def rowmap(col):
        return lambda b, i: (base + b * nb + i, col)

    out_map = lambda b, i: (b * nb + i, 0)
    n = grp.total
    return pl.pallas_call(
        _prep_ac_kernel,
        grid=(grp.bsz, nb),
        in_specs=[
            pl.BlockSpec((tr, a_w), rowmap(0)),
            pl.BlockSpec((tr, C_W), rowmap(COL_CQ // C_W)),
            pl.BlockSpec((tr, C_W), rowmap(COL_CK // C_W)),
            pl.BlockSpec((tr, C_W), rowmap(COL_CV // C_W)),
            pl.BlockSpec((tr, HEAD_DIM), lambda b, i: (i, 0)),
            pl.BlockSpec((tr, HEAD_DIM), lambda b, i: (i, 0)),
            pl.BlockSpec((1, HEAD_DIM), lambda b, i: (0, 0)),
            pl.BlockSpec((1, HEAD_DIM), lambda b, i: (0, 0)),
        ],
        out_specs=[
            pl.BlockSpec((tr, A_Q_W), out_map),
            pl.BlockSpec((tr, A_KV_W), out_map),
            pl.BlockSpec((tr, A_KV_W), out_map),
            pl.BlockSpec((tr, C_W), out_map),
            pl.BlockSpec((tr, 2 * C_W), out_map),
        ],
        out_shape=[
            jax.ShapeDtypeStruct((n, A_Q_W), BF16),
            jax.ShapeDtypeStruct((n, A_KV_W), BF16),
            jax.ShapeDtypeStruct((n, A_KV_W), BF16),
            jax.ShapeDtypeStruct((n, C_W), BF16),
            jax.ShapeDtypeStruct((n, 2 * C_W), BF16),
        ],
        compiler_params=_params(("parallel", "parallel")),
        name="prep_ac",
    )(proj, proj, proj, proj, cos, sin, q_gain.reshape(1, HEAD_DIM), k_gain.reshape(1, HEAD_DIM))


FLASH_CK = 256
FLASH_UNROLL = 4


def _flash_kernel(q_ref, k_ref, v_ref, bias_ref, mix_ref, o_ref, q2_ref, s_ref, m_ref, l_ref, acc_ref, *, tq, nchunks):
    q2_ref[:tq] = q_ref[:, :HEAD_DIM]
    q2_ref[tq:] = q_ref[:, HEAD_DIM:]
    q2 = q2_ref[...]

    def scores(t, slot):
        start = pl.multiple_of(t * FLASH_CK, FLASH_CK)
        k = k_ref[pl.ds(start, FLASH_CK), :]
        s_ref[slot] = lax.dot_general(q2, k, (((1,), (1,)), ((), ())), preferred_element_type=F32)

    def step(t, slot, masked, prefetch):
        if prefetch:
            scores(t + 1, 1 - slot)

        def read_scores():
            s = s_ref[slot]
            return s + bias_ref[pl.ds(t, 1), :] if masked else s

        m = m_ref[...]
        m_new = jnp.maximum(m, jnp.max(read_scores(), axis=-1, keepdims=True))
        m_ref[...] = m_new
        p = jnp.exp2(read_scores() - jnp.concatenate([m_new, m_new], axis=-1))
        alpha = jnp.exp2(m - m_new)
        v = v_ref[pl.ds(pl.multiple_of(t * FLASH_CK, FLASH_CK), FLASH_CK), :]
        pv = jnp.dot(p.astype(BF16), v, preferred_element_type=F32)
        l_ref[...] = alpha * l_ref[...] + pv[:, HEAD_DIM:]
        acc_ref[...] = alpha * acc_ref[...] + pv[:, :HEAD_DIM]

    m_ref[...] = jnp.full(m_ref.shape, -jnp.inf, F32)
    l_ref[...] = jnp.zeros(l_ref.shape, F32)
    acc_ref[...] = jnp.zeros(acc_ref.shape, F32)
    scores(0, 0)
    step(0, 0, True, nchunks > 1)
    if nchunks > 1:
        t0 = 1
        for _ in range((nchunks - 2) % FLASH_UNROLL):
            step(t0, t0 % 2, False, True)
            t0 += 1

        def group(j, carry):
            for u in range(FLASH_UNROLL):
                step(t0 + FLASH_UNROLL * j + u, (t0 + u) % 2, False, True)
            return carry

        lax.fori_loop(0, (nchunks - 1 - t0) // FLASH_UNROLL, group, 0)
        step(nchunks - 1, (nchunks - 1) % 2, True, False)
    o = acc_ref[...] / l_ref[...]
    o_ref[:, :HEAD_DIM] = o[:tq].astype(o_ref.dtype)
    o_ref[:, HEAD_DIM:] = o[tq:].astype(o_ref.dtype)


def flash_attention(qa, ka, va, key_bias, mix, grp):
    lp = grp.lp
    lkv = ka.shape[0] // grp.bsz
    assert lkv % FLASH_CK == 0 and lkv - grp.valid_end <= FLASH_CK and HEAD_PAD <= FLASH_CK
    tq = _pick_tile(lp, 384, 128)
    nchunks = lkv // FLASH_CK
    nq = lp // tq
    gw = 2 * HEAD_DIM
    return pl.pallas_call(
        functools.partial(_flash_kernel, tq=tq, nchunks=nchunks),
        grid=(grp.bsz, A_KV_HEADS, nq),
        in_specs=[
            pl.BlockSpec((tq, gw), lambda b, g, i: (b * nq + i, g)),
            pl.BlockSpec((lkv, HEAD_DIM), lambda b, g, i: (b, g)),
            pl.BlockSpec((lkv, 2 * HEAD_DIM), lambda b, g, i: (b, g)),
            pl.BlockSpec((nchunks, FLASH_CK), lambda b, g, i: (0, 0)),
            pl.BlockSpec(memory_space=pl.ANY),
        ],
        out_specs=pl.BlockSpec((tq, gw), lambda b, g, i: (grp.base // tq + b * nq + i, MIX_COL_A // gw + g)),
        out_shape=jax.ShapeDtypeStruct(mix.shape, mix.dtype),
        input_output_aliases={4: 0},
        scratch_shapes=[
            pltpu.VMEM((2 * tq, HEAD_DIM), BF16),
            pltpu.VMEM((2, 2 * tq, FLASH_CK), F32),
            pltpu.VMEM((2 * tq, HEAD_DIM), F32),
            pltpu.VMEM((2 * tq, HEAD_DIM), F32),
            pltpu.VMEM((2 * tq, HEAD_DIM), F32),
        ],
        compiler_params=_params(("parallel", "parallel", "parallel")),
        name="flash_attention",
    )(qa, ka, va, key_bias.reshape(nchunks, FLASH_CK), mix)


def values_with_ones(va):
    v = va.reshape(va.shape[0], A_KV_HEADS, HEAD_DIM)
    return jnp.concatenate([v, jnp.ones_like(v)], axis=-1).reshape(va.shape[0], 2 * A_KV_W)


def pad_keys(x, grp, value=0):
    lkv = -(-grp.lp // FLASH_CK) * FLASH_CK
    x = x.reshape(grp.bsz, grp.lp, -1)
    x = jnp.pad(x, ((0, 0), (0, lkv - grp.lp), (0, 0)), constant_values=value)
    return x.reshape(grp.bsz * lkv, -1)


NA_KEYS = BLK + WIN_ROWS * GRID_W
NA_META_VARIANT = WIN_ROWS


def _na_bias_table(rel_bias, meta_bias):
    rel_bias = rel_bias.astype(F32)
    qc = jnp.arange(GRID_W, dtype=jnp.int32)
    c = jnp.arange(GRID_W, dtype=jnp.int32)
    col_start = jnp.clip(qc - WIN_COLS // 2, 0, GRID_W - WIN_COLS)
    in_win = (c[None, :] >= col_start[:, None]) & (c[None, :] < col_start[:, None] + WIN_COLS)
    dc = jnp.clip(c[None, :] - qc[:, None] + (WIN_COLS - 1), 0, 2 * WIN_COLS - 2)
    full = jnp.where(in_win[None, None], rel_bias[:, :, dc], NEG)
    r = jnp.arange(WIN_ROWS, dtype=jnp.int32)
    variants = []
    for d0 in range(WIN_ROWS):
        band = full[:, d0 + r]
        variants.append(jnp.moveaxis(band, 1, 2).reshape(C_HEADS, GRID_W, WIN_ROWS * GRID_W))
    meta_band = jnp.broadcast_to(variants[WIN_ROWS - 1][:, :1], (C_HEADS, GRID_W, WIN_ROWS * GRID_W))
    variants.append(meta_band)
    band = jnp.stack(variants)
    mcol = jnp.concatenate([jnp.full((C_HEADS, HEAD_PAD), NEG, F32), meta_bias.astype(F32)], axis=-1)
    mcol = jnp.broadcast_to(mcol[None, :, None, :], (WIN_ROWS + 1, C_HEADS, GRID_W, BLK))
    return jnp.concatenate([mcol, band], axis=-1)


def _na_row_start(j, rows):
    return jnp.clip(j - 1 - WIN_ROWS // 2, 0, rows - WIN_ROWS)


NA_PAIR = 2


def _na_kernel(q_ref, meta_ref, *rest, rows):
    band_refs = rest[:NA_PAIR * WIN_ROWS]
    bias_ref, _, o_ref = rest[NA_PAIR * WIN_ROWS:]
    meta = meta_ref[...]
    items = []
    for u in range(NA_PAIR):
        j = jnp.minimum(pl.program_id(1) * NA_PAIR + u, rows)
        variant = jnp.where(j == 0, NA_META_VARIANT, _na_row_start(j, rows) - (j - 1) + (WIN_ROWS - 1))
        kv = jnp.concatenate([meta] + [r[...] for r in band_refs[u * WIN_ROWS:(u + 1) * WIN_ROWS]], axis=0)
        for h in range(C_HEADS):
            items.append((slice(u * BLK, (u + 1) * BLK), slice(h * HEAD_DIM, (h + 1) * HEAD_DIM), kv, variant, h))
    s = [lax.dot_general(q_ref[rs, sl], kv[:, sl], (((1,), (1,)), ((), ())), preferred_element_type=F32)
         + bias_ref[variant, h] for rs, sl, kv, variant, h in items]
    e = [jnp.exp(x - jnp.max(x, axis=-1, keepdims=True)) for x in s]
    p = [(x / jnp.sum(x, axis=-1, keepdims=True)).astype(BF16) for x in e]
    o = [jnp.dot(p[n], kv[:, C_W + sl.start:C_W + sl.stop], preferred_element_type=F32)
         for n, (rs, sl, kv, variant, h) in enumerate(items)]
    for n, (rs, sl, kv, variant, h) in enumerate(items):
        o_ref[rs, sl] = o[n].astype(o_ref.dtype)


def neighbourhood_attention(qc, kvc, bias_table, mix, grp):
    nblk = grp.lp // BLK
    rows = grp.rows
    assert nblk % NA_PAIR == 0
    npair = nblk // NA_PAIR
    qrows = NA_PAIR * BLK

    def band_map(u, r):
        def index_map(b, j):
            jj = jnp.minimum(j * NA_PAIR + u, rows)
            return (b * nblk + 1 + _na_row_start(jj, rows) + r, 0)
        return index_map

    in_specs = [
        pl.BlockSpec((qrows, C_W), lambda b, j: (b * npair + j, 0)),
        pl.BlockSpec((BLK, 2 * C_W), lambda b, j: (b * nblk, 0)),
    ]
    in_specs += [pl.BlockSpec((BLK, 2 * C_W), band_map(u, r)) for u in range(NA_PAIR) for r in range(WIN_ROWS)]
    in_specs += [pl.BlockSpec(bias_table.shape, lambda b, j: (0, 0, 0, 0)), pl.BlockSpec(memory_space=pl.ANY)]
    return pl.pallas_call(
        functools.partial(_na_kernel, rows=rows),
        grid=(grp.bsz, npair),
        in_specs=in_specs,
        out_specs=pl.BlockSpec((qrows, C_W), lambda b, j: (grp.base // qrows + b * npair + j, MIX_COL_C // C_W)),
        out_shape=jax.ShapeDtypeStruct(mix.shape, mix.dtype),
        input_output_aliases={len(in_specs) - 1: 0},
        compiler_params=_params(("parallel", "parallel")),
        name="neighbourhood_attention",
    )(qc, kvc, *([kvc] * (NA_PAIR * WIN_ROWS)), bias_table, mix)


HALO = 8
GATE_G = 0
GATE_BETA = 2 * B_HEADS


def _dn_prep_kernel(xp_ref, xc_ref, xn_ref, gate_ref, w_ref, a_ref, dtb_ref, qkv_ref, gb_ref, win_ref,
                    *, tb, valid_end):
    i = pl.program_id(1)
    part = pl.program_id(2)
    pos_w = i * tb - HALO + lax.broadcasted_iota(jnp.int32, (tb + 2 * HALO, 1), 0)
    ok_w = (pos_w >= HEAD_PAD) & (pos_w < valid_end)
    win_ref[0:HALO] = xp_ref[...]
    win_ref[HALO:HALO + tb] = xc_ref[...]
    win_ref[HALO + tb:] = xn_ref[...]
    win_ref[...] = jnp.where(ok_w, win_ref[...], 0.0)
    y = jnp.zeros((tb, B_W), F32)
    for j in range(CONV_K):
        y = y + win_ref[pl.ds(HALO - CONV_K // 2 + j, tb), :] * w_ref[j:j + 1, :]
    y = y * jax.nn.sigmoid(y)
    pos = i * tb + lax.broadcasted_iota(jnp.int32, (tb, 1), 0)
    ok = (pos >= HEAD_PAD) & (pos < valid_end)
    unit = jnp.where(part == 0, HEAD_DIM ** -0.5, 1.0)
    for h in range(B_HEADS):
        yh = y[:, h * HEAD_DIM:(h + 1) * HEAD_DIM]
        inv = lax.rsqrt(jnp.sum(yh * yh, axis=-1, keepdims=True) + EPS) * unit
        fac = jnp.where(part == 2, 1.0, inv)
        qkv_ref[:, h * HEAD_DIM:(h + 1) * HEAD_DIM] = jnp.where(ok, yh * fac, 0.0)

    @pl.when(part == 0)
    def _():
        x = gate_ref[...]
        lane = lax.broadcasted_iota(jnp.int32, x.shape, 1)
        z = x + dtb_ref[...]
        softplus = jnp.maximum(z, 0.0) + jnp.log1p(jnp.exp(-jnp.abs(z)))
        g = -jnp.exp(a_ref[...]) * softplus
        beta = jax.nn.sigmoid(x)
        out = jnp.where(lane < GATE_BETA, g, jnp.where(lane < 2 * GATE_BETA, beta, 0.0))
        gb_ref[...] = jnp.where(ok, out, 0.0)


def dn_prep(proj, grp, conv_w, a_log, dt_bias):
    tb = grp.row_tile(384)
    nb = grp.lp // tb
    base = grp.base // tb
    hb = tb // HALO
    last_halo = proj.shape[0] // HALO - 1
    qkv_col = COL_BQKV // B_W
    pad = jnp.zeros((1, 128 - 2 * B_HEADS), F32)
    a_row = jnp.concatenate([a_log.astype(F32).reshape(1, 2 * B_HEADS), pad], axis=-1)
    dtb_row = jnp.concatenate([dt_bias.astype(F32).reshape(1, 2 * B_HEADS), pad], axis=-1)
    return pl.pallas_call(
        functools.partial(_dn_prep_kernel, tb=tb, valid_end=grp.valid_end),
        grid=(grp.bsz, nb, 3),
        in_specs=[
            pl.BlockSpec((HALO, B_W), lambda b, i, p: (jnp.maximum((base + b * nb + i) * hb - 1, 0), qkv_col + p)),
            pl.BlockSpec((tb, B_W), lambda b, i, p: (base + b * nb + i, qkv_col + p)),
            pl.BlockSpec((HALO, B_W), lambda b, i, p: (jnp.minimum((base + b * nb + i + 1) * hb, last_halo), qkv_col + p)),
            pl.BlockSpec((tb, 128), lambda b, i, p: (base + b * nb + i, COL_GATE // 128)),
            pl.BlockSpec((CONV_K, B_W), lambda b, i, p: (0, p)),
            pl.BlockSpec((1, 128), lambda b, i, p: (0, 0)),
            pl.BlockSpec((1, 128), lambda b, i, p: (0, 0)),
        ],
        out_specs=[
            pl.BlockSpec((tb, B_W), lambda b, i, p: (b * nb + i, p)),
            pl.BlockSpec((tb, 128), lambda b, i, p: (b * nb + i, 0)),
        ],
        out_shape=[
            jax.ShapeDtypeStruct((grp.total, 3 * B_W), F32),
            jax.ShapeDtypeStruct((grp.total, 128), F32),
        ],
        scratch_shapes=[pltpu.VMEM((tb + 2 * HALO, B_W), F32)],
        compiler_params=_params(("parallel", "parallel", "arbitrary")),
        name="dn_prep",
    )(proj, proj, proj, proj, conv_w, a_row, dtb_row)


def _nt(a, b):
    return lax.dot_general(a, b, (((1,), (1,)), ((), ())), preferred_element_type=F32)


def _dn_scan_kernel(qf_ref, kf_ref, vf_ref, gf_ref, qb_ref, kb_ref, vb_ref, gb_ref, of_ref, ob_ref, s_ref):
    @pl.when(pl.program_id(1) == 0)
    def _():
        s_ref[...] = jnp.zeros_like(s_ref)

    ri = lax.broadcasted_iota(jnp.int32, (CHUNK, CHUNK), 0)
    ci = lax.broadcasted_iota(jnp.int32, (CHUNK, CHUNK), 1)
    eye = (ri == ci).astype(F32)
    eye128 = (lax.broadcasted_iota(jnp.int32, (128, 128), 0) == lax.broadcasted_iota(jnp.int32, (128, 128), 1)).astype(F32)
    dirs = ((qf_ref, kf_ref, vf_ref, gf_ref, of_ref), (qb_ref, kb_ref, vb_ref, gb_ref, ob_ref))
    n = 2 * B_HEADS
    incl, strict, gcc, gcr, g_tot, beta, refs = [], [], [], [], [], [], []
    for d, (q_ref, k_ref, v_ref, g_ref, o_ref) in enumerate(dirs):
        incl_d = (ri >= ci) if d == 0 else (ri <= ci)
        strict_d = (ri > ci) if d == 0 else (ri < ci)
        gates = g_ref[...]
        cum = jnp.dot(incl_d.astype(F32), gates, preferred_element_type=F32, precision=lax.Precision.HIGHEST)
        cum_t = lax.dot_general(eye128, cum, (((1,), (1,)), ((), ())), preferred_element_type=F32,
                                precision=lax.Precision.HIGHEST)
        last = CHUNK - 1 if d == 0 else 0
        for h in range(B_HEADS):
            c = GATE_G + d * B_HEADS + h
            sl = slice(h * HEAD_DIM, (h + 1) * HEAD_DIM)
            incl.append(incl_d)
            strict.append(strict_d)
            gcc.append(cum[:, c:c + 1])
            gcr.append(cum_t[c:c + 1, :])
            g_tot.append(cum_t[c:c + 1, last:last + 1])
            beta.append(gates[:, GATE_BETA + c:GATE_BETA + c + 1])
            refs.append((q_ref, k_ref, v_ref, o_ref, sl))

    decay = [jnp.exp(jnp.where(incl[i], gcc[i] - gcr[i], -jnp.inf)) for i in range(n)]
    k = [refs[i][1][:, refs[i][4]] for i in range(n)]
    kb = [k[i] * beta[i] for i in range(n)]
    k16 = [k[i].astype(BF16) for i in range(n)]
    q = [refs[i][0][:, refs[i][4]] for i in range(n)]
    kk_qk = [_nt(jnp.concatenate([kb[i], q[i]], axis=0).astype(BF16), k16[i]) for i in range(n)]
    x = [-jnp.where(strict[i], kk_qk[i][:CHUNK] * decay[i], 0.0) for i in range(n)]
    qk = [jnp.where(incl[i], kk_qk[i][CHUNK:] * decay[i], 0.0).astype(BF16) for i in range(n)]
    t = [x[i] + eye for i in range(n)]
    p = x
    for _ in range(5):
        p16 = [p[i].astype(BF16) for i in range(n)]
        p = [jnp.dot(p16[i], p16[i], preferred_element_type=F32) for i in range(n)]
        t = [t[i] + jnp.dot(t[i].astype(BF16), p[i].astype(BF16), preferred_element_type=F32) for i in range(n)]
    egc = [jnp.exp(gcc[i]) for i in range(n)]
    rhs = [jnp.concatenate([kb[i] * egc[i], refs[i][2][:, refs[i][4]] * beta[i]], axis=-1).astype(BF16) for i in range(n)]
    wu = [jnp.dot(t[i].astype(BF16), rhs[i], preferred_element_type=F32) for i in range(n)]
    s = [s_ref[i] for i in range(n)]
    lhs = [jnp.concatenate([wu[i][:, :HEAD_DIM], q[i] * egc[i]], axis=0).astype(BF16) for i in range(n)]
    both = [jnp.dot(lhs[i], s[i].astype(BF16), preferred_element_type=F32) for i in range(n)]
    v16 = [(wu[i][:, HEAD_DIM:] - both[i][:CHUNK]).astype(BF16) for i in range(n)]
    for i in range(n):
        o_ref, sl = refs[i][3], refs[i][4]
        o_ref[:, sl] = both[i][CHUNK:] + jnp.dot(qk[i], v16[i], preferred_element_type=F32)
    kd = [(k[i] * jnp.exp(g_tot[i] - gcc[i])).astype(BF16) for i in range(n)]
    for i in range(n):
        s_ref[i] = s[i] * jnp.exp(g_tot[i]) + lax.dot_general(
            kd[i], v16[i], (((0,), (0,)), ((), ())), preferred_element_type=F32)


def dn_scan(qkv, gb, grp):
    nc = grp.lp // BLK

    def fw(col):
        return lambda b, c: (b * nc + c, col)

    def bw(col):
        return lambda b, c: (b * nc + nc - 1 - c, col)

    blk = (BLK, B_W)
    return pl.pallas_call(
        _dn_scan_kernel,
        grid=(grp.bsz, nc),
        in_specs=[
            pl.BlockSpec(blk, fw(0)), pl.BlockSpec(blk, fw(1)), pl.BlockSpec(blk, fw(2)), pl.BlockSpec((BLK, 128), fw(0)),
            pl.BlockSpec(blk, bw(0)), pl.BlockSpec(blk, bw(1)), pl.BlockSpec(blk, bw(2)), pl.BlockSpec((BLK, 128), bw(0)),
        ],
        out_specs=[pl.BlockSpec(blk, fw(0)), pl.BlockSpec(blk, bw(0))],
        out_shape=[jax.ShapeDtypeStruct((grp.total, B_W), F32)] * 2,
        scratch_shapes=[pltpu.VMEM((2 * B_HEADS, HEAD_DIM, HEAD_DIM), F32)],
        compiler_params=_params(("parallel", "arbitrary")),
        name="dn_scan",
    )(qkv, qkv, qkv, gb, qkv, qkv, qkv, gb)


def _dn_post_kernel(of_ref, ob_ref, z_ref, gain_ref, mix_ref, o_ref):
    for h in range(B_HEADS):
        lo, hi = h * HEAD_DIM, (h + 1) * HEAD_DIM
        o = of_ref[:, lo:hi] + ob_ref[:, lo:hi]
        ms = jnp.mean(o * o, axis=-1, keepdims=True)
        z = z_ref[:, lo:hi]
        o_ref[:, lo:hi] = (o * lax.rsqrt(ms + EPS) * gain_ref[...] * (z * jax.nn.sigmoid(z))).astype(o_ref.dtype)


def dn_post(o_fw, o_bw, proj, mix, grp, out_gain):
    tb = grp.row_tile(384)
    nb = grp.lp // tb
    base = grp.base // tb
    return pl.pallas_call(
        _dn_post_kernel,
        grid=(grp.bsz * nb,),
        in_specs=[
            pl.BlockSpec((tb, B_W), lambda i: (i, 0)),
            pl.BlockSpec((tb, B_W), lambda i: (i, 0)),
            pl.BlockSpec((tb, B_W), lambda i: (base + i, COL_BZ // B_W)),
            pl.BlockSpec((1, HEAD_DIM), lambda i: (0, 0)),
            pl.BlockSpec(memory_space=pl.ANY),
        ],
        out_specs=pl.BlockSpec((tb, B_W), lambda i: (base + i, MIX_COL_B // B_W)),
        out_shape=jax.ShapeDtypeStruct(mix.shape, mix.dtype),
        input_output_aliases={4: 0},
        compiler_params=_params(("parallel",)),
        name="dn_post",
    )(o_fw, o_bw, proj, out_gain.reshape(1, HEAD_DIM), mix)


def _permute_w_in(w_in):
    sizes = (A_Q_W, A_KV_W, A_KV_W, 3 * B_W, B_W, 2 * B_HEADS, 2 * B_HEADS, C_W, C_W, C_W)
    offs = [0]
    for s in sizes:
        offs.append(offs[-1] + s)
    seg = [w_in[..., offs[n]:offs[n + 1]] for n in range(len(sizes))]
    aq, ak, av, bqkv, bz, ba, bb, cq, ck, cv = seg
    used = COL_GATE + 4 * B_HEADS
    pad = jnp.zeros(w_in.shape[:-1] + (IN_WIDTH_PAD - used,), w_in.dtype)
    return jnp.concatenate([aq, ak, av, bqkv, bz, cq, ck, cv, ba, bb, pad], axis=-1).astype(BF16)


def _stack_tokens(xs, groups, meta_tokens):
    parts = []
    for x, grp in zip(xs, groups):
        d = x.shape[-1]
        meta = jnp.broadcast_to(meta_tokens.astype(x.dtype)[None], (grp.bsz, N_META, d))
        front = jnp.zeros((grp.bsz, HEAD_PAD, d), x.dtype)
        tail = jnp.zeros((grp.bsz, grp.lp - grp.valid_end, d), x.dtype)
        parts.append(jnp.concatenate([front, meta, x, tail], axis=1).reshape(grp.total, d))
    return jnp.concatenate(parts, axis=0)


def kernel(x_prompt, x_sample, meta_tokens, ffn1_norm, ffn1_w_gate, ffn1_w_up, ffn1_w_down, mix_norm, w_in,
           attn_q_norm, attn_k_norm, dn_conv_w, dn_a_log, dn_dt_bias, dn_out_norm, na_rel_bias, na_meta_bias,
           w_out, ffn2_norm, ffn2_w_gate, ffn2_w_up, ffn2_w_down, final_norm):
    depth = w_in.shape[0]
    g_prompt = Group(x_prompt.shape[0], x_prompt.shape[1], 0)
    g_sample = Group(x_sample.shape[0], x_sample.shape[1], g_prompt.total)
    groups = (g_prompt, g_sample)
    x = _stack_tokens((x_prompt, x_sample), groups, meta_tokens)
    tables = [_position_tables(grp) for grp in groups]

    w1g, w1u, w1d = (w.astype(BF16) for w in (ffn1_w_gate, ffn1_w_up, ffn1_w_down))
    w2g, w2u, w2d = (w.astype(BF16) for w in (ffn2_w_gate, ffn2_w_up, ffn2_w_down))
    w_in_p = _permute_w_in(w_in)
    w_out_b = jnp.concatenate([w_out[:, A_Q_W:A_Q_W + B_W], w_out[:, :A_Q_W], w_out[:, A_Q_W + B_W:]], axis=1).astype(BF16)
    mix = jnp.zeros((x.shape[0], MIX_WIDTH), BF16)

    for l in range(depth):
        x = residual_matmul(x, norm_gate_up(x, ffn1_norm[l], w1g, w1u, l), w1d, l, 0.5)

        proj = norm_matmul(x, mix_norm[l], w_in_p, l, IN_TILE, F32)
        bias_table = _na_bias_table(na_rel_bias[l], na_meta_bias[l])
        for grp, (cos, sin, key_bias) in zip(groups, tables):
            qa, ka, va, qc, kvc = prep_ac(proj, grp, cos, sin, attn_q_norm[l], attn_k_norm[l])
            mix = flash_attention(qa, pad_keys(ka, grp), values_with_ones(pad_keys(va, grp)), key_bias, mix, grp)
            mix = neighbourhood_attention(qc, kvc, bias_table, mix, grp)
            qkv, gb = dn_prep(proj, grp, dn_conv_w[l], dn_a_log[l], dn_dt_bias[l])
            o_fw, o_bw = dn_scan(qkv, gb, grp)
            mix = dn_post(o_fw, o_bw, proj, mix, grp, dn_out_norm[l])
        x = residual_matmul(x, mix, w_out_b, l, 1.0)

        x = residual_matmul(x, norm_gate_up(x, ffn2_norm[l], w2g, w2u, l), w2d, l, 0.5)

    return tuple(final_rmsnorm(x, final_norm, grp) for grp in groups)
```

```python
import functools

import jax
import jax.numpy as jnp
from jax import lax
from jax.experimental import pallas as pl
from jax.experimental.pallas import tpu as pltpu

F32 = jnp.float32
BF16 = jnp.bfloat16

HEAD_DIM = 128
N_META = 16
GRID_W = 64
EPS = 1e-6
A_HEADS = 4
A_KV_HEADS = 2
ROPE_THETA = 10000.0
B_HEADS = 8
CONV_K = 5
CHUNK = 64
C_HEADS = 4
WIN_ROWS = 8
WIN_COLS = 16

A_Q_W = A_HEADS * HEAD_DIM
A_KV_W = A_KV_HEADS * HEAD_DIM
B_W = B_HEADS * HEAD_DIM
C_W = C_HEADS * HEAD_DIM
MIX_WIDTH = A_Q_W + B_W + C_W

BLK = 64
HEAD_PAD = BLK - N_META
SEQ_ALIGN = 128
NEG = -1e30
LOG2E = 1.4426950408889634

COL_AQ = 0
COL_AK = COL_AQ + A_Q_W
COL_AV = COL_AK + A_KV_W
COL_BQKV = COL_AV + A_KV_W
COL_BZ = COL_BQKV + 3 * B_W
COL_CQ = COL_BZ + B_W
COL_CK = COL_CQ + C_W
COL_CV = COL_CK + C_W
COL_GATE = COL_CV + C_W
MIX_COL_B = 0
MIX_COL_A = MIX_COL_B + B_W
MIX_COL_C = MIX_COL_A + A_Q_W
IN_TILE = 768
IN_WIDTH_PAD = -(-(COL_GATE + 128) // IN_TILE) * IN_TILE

V7X_VMEM_BYTES = 64 * 1024 * 1024
VMEM_LIMIT = 52 * 1024 * 1024


def _params(sem):
    return pltpu.CompilerParams(dimension_semantics=sem, vmem_limit_bytes=VMEM_LIMIT)


def _pick_tile(n, target, mult):
    best = None
    for t in range(mult, min(n, target) + 1, mult):
        if n % t == 0:
            best = t
    assert best is not None, (n, target, mult)
    return best


def _rmsnorm_kernel(x_ref, g_ref, o_ref):
    x = x_ref[...]
    ms = jnp.mean(x * x, axis=-1, keepdims=True)
    o_ref[...] = (x * lax.rsqrt(ms + EPS) * g_ref[...]).astype(o_ref.dtype)


NORM_SUB_ROWS = 16
NORM_UNROLL = 4


def _norm_rows(x_ref, g_ref, h_ref):
    @pl.when(pl.program_id(1) == 0)
    def _():
        rows = x_ref.shape[0]
        sub = NORM_SUB_ROWS
        unroll = max(u for u in range(1, NORM_UNROLL + 1) if rows % (sub * u) == 0)

        def body(r, carry):
            for u in range(unroll):
                rs = pl.ds(pl.multiple_of((r * unroll + u) * sub, sub), sub)
                x = x_ref[rs, :]
                ms = jnp.mean(x * x, axis=-1, keepdims=True)
                h_ref[rs, :] = (x * lax.rsqrt(ms + EPS) * g_ref[...]).astype(h_ref.dtype)
            return carry

        lax.fori_loop(0, rows // (sub * unroll), body, 0)


def _norm_gate_up_kernel(x_ref, g_ref, wg_ref, wu_ref, o_ref, h_ref):
    _norm_rows(x_ref, g_ref, h_ref)
    h = h_ref[...]
    g = jnp.dot(h, wg_ref[...], preferred_element_type=F32)
    u = jnp.dot(h, wu_ref[...], preferred_element_type=F32)
    o_ref[...] = (g * jax.nn.sigmoid(g) * u).astype(o_ref.dtype)


def norm_gate_up(x, gain, wg, wu, layer):
    t, d = x.shape
    f = wg.shape[-1]
    tm = _pick_tile(t, 640, 16)
    tn = _pick_tile(f, 512, 128)
    return pl.pallas_call(
        _norm_gate_up_kernel,
        grid=(t // tm, f // tn),
        in_specs=[
            pl.BlockSpec((tm, d), lambda i, j: (i, 0)),
            pl.BlockSpec((1, d), lambda i, j: (0, 0)),
            pl.BlockSpec((None, d, tn), lambda i, j: (layer, 0, j)),
            pl.BlockSpec((None, d, tn), lambda i, j: (layer, 0, j)),
        ],
        out_specs=pl.BlockSpec((tm, tn), lambda i, j: (i, j)),
        out_shape=jax.ShapeDtypeStruct((t, f), BF16),
        scratch_shapes=[pltpu.VMEM((tm, d), BF16)],
        compiler_params=_params(("parallel", "arbitrary")),
        name="norm_gate_up",
    )(x, gain.reshape(1, d), wg, wu)


def _norm_matmul_kernel(x_ref, g_ref, w_ref, o_ref, h_ref):
    _norm_rows(x_ref, g_ref, h_ref)
    o_ref[...] = jnp.dot(h_ref[...], w_ref[...], preferred_element_type=F32).astype(o_ref.dtype)


def norm_matmul(x, gain, w, layer, tn_target, out_dtype):
    t, d = x.shape
    n = w.shape[-1]
    tm = _pick_tile(t, 640, 16)
    tn = _pick_tile(n, tn_target, 128)
    return pl.pallas_call(
        _norm_matmul_kernel,
        grid=(t // tm, n // tn),
        in_specs=[
            pl.BlockSpec((tm, d), lambda i, j: (i, 0)),
            pl.BlockSpec((1, d), lambda i, j: (0, 0)),
            pl.BlockSpec((None, d, tn), lambda i, j: (layer, 0, j)),
        ],
        out_specs=pl.BlockSpec((tm, tn), lambda i, j: (i, j)),
        out_shape=jax.ShapeDtypeStruct((t, n), out_dtype),
        scratch_shapes=[pltpu.VMEM((tm, d), BF16)],
        compiler_params=_params(("parallel", "arbitrary")),
        name="norm_matmul",
    )(x, gain.reshape(1, d), w)


def _residual_matmul_kernel(x_ref, a_ref, w_ref, o_ref, *, scale):
    o_ref[...] = x_ref[...] + scale * jnp.dot(a_ref[...], w_ref[...], preferred_element_type=F32)


def residual_matmul(x, a, w, layer, scale):
    t, d = x.shape
    k = a.shape[-1]
    tm = _pick_tile(t, 640, 16)
    tn = _pick_tile(d, 1024, 128)
    return pl.pallas_call(
        functools.partial(_residual_matmul_kernel, scale=scale),
        grid=(d // tn, t // tm),
        in_specs=[
            pl.BlockSpec((tm, tn), lambda j, i: (i, j)),
            pl.BlockSpec((tm, k), lambda j, i: (i, 0)),
            pl.BlockSpec((None, k, tn), lambda j, i: (layer, 0, j)),
        ],
        out_specs=pl.BlockSpec((tm, tn), lambda j, i: (i, j)),
        out_shape=jax.ShapeDtypeStruct((t, d), F32),
        compiler_params=_params(("parallel", "parallel")),
        name="residual_matmul",
    )(x, a, w)


def final_rmsnorm(x, gain, grp):
    d = x.shape[-1]
    nblk = grp.lp // BLK
    base = grp.base // BLK
    rows = grp.n_tok // BLK
    return pl.pallas_call(
        _rmsnorm_kernel,
        grid=(grp.bsz, rows),
        in_specs=[
            pl.BlockSpec((BLK, d), lambda b, i: (base + b * nblk + 1 + i, 0)),
            pl.BlockSpec((1, d), lambda b, i: (0, 0)),
        ],
        out_specs=pl.BlockSpec((None, BLK, d), lambda b, i: (b, i, 0)),
        out_shape=jax.ShapeDtypeStruct((grp.bsz, grp.n_tok, d), F32),
        compiler_params=_params(("parallel", "parallel")),
        name="final_norm",
    )(x, gain.reshape(1, d))


class Group:
    def __init__(self, bsz, n_tok, base):
        assert n_tok % GRID_W == 0 and n_tok // GRID_W >= WIN_ROWS
        self.bsz = bsz
        self.n_tok = n_tok
        self.lp = -(-(BLK + n_tok) // SEQ_ALIGN) * SEQ_ALIGN
        self.base = base
        self.rows = n_tok // GRID_W
        self.valid_end = BLK + n_tok

    @property
    def total(self):
        return self.bsz * self.lp

    def row_tile(self, target):
        best = SEQ_ALIGN
        for t in range(SEQ_ALIGN, target + 1, SEQ_ALIGN):
            if self.lp % t == 0 and self.base % t == 0:
                best = t
        return best


def _position_tables(grp):
    p = jnp.arange(grp.lp, dtype=jnp.int32)
    t = jnp.clip(p - BLK, 0, grp.n_tok - 1)
    is_tok = (p >= BLK) & (p < grp.valid_end)
    row = jnp.where(is_tok, t // GRID_W, 0).astype(F32)
    col = jnp.where(is_tok, t % GRID_W, 0).astype(F32)
    half = HEAD_DIM // 4
    freqs = ROPE_THETA ** (-jnp.arange(half, dtype=F32) / half)
    ang_r = row[:, None] * freqs[None, :]
    ang_c = col[:, None] * freqs[None, :]
    cos = jnp.concatenate([jnp.cos(ang_r), jnp.cos(ang_r), jnp.cos(ang_c), jnp.cos(ang_c)], axis=-1)
    sin = jnp.concatenate([-jnp.sin(ang_r), jnp.sin(ang_r), -jnp.sin(ang_c), jnp.sin(ang_c)], axis=-1)
    pk = jnp.arange(-(-grp.lp // FLASH_CK) * FLASH_CK, dtype=jnp.int32)
    key_bias = jnp.where((pk >= HEAD_PAD) & (pk < grp.valid_end), 0.0, NEG).astype(F32)
    return cos, sin, key_bias


def _swap_halves(x):
    lane = lax.broadcasted_iota(jnp.int32, x.shape, 1)
    return jnp.where(lane % 64 < 32, pltpu.roll(x, 96, 1), pltpu.roll(x, 32, 1))


def _prep_ac_kernel(a_ref, cq_ref, ck_ref, cv_ref, cos_ref, sin_ref, qg_ref, kg_ref,
                    qa_ref, ka_ref, va_ref, qc_ref, kvc_ref):
    cos = cos_ref[...]
    sin = sin_ref[...]
    scale = HEAD_DIM ** -0.5

    def norm_rope(x, gain):
        ms = jnp.mean(x * x, axis=-1, keepdims=True)
        y = x * lax.rsqrt(ms + EPS) * gain
        return y * cos + _swap_halves(y) * sin

    for h in range(A_HEADS):
        x = a_ref[:, COL_AQ + h * HEAD_DIM:COL_AQ + (h + 1) * HEAD_DIM]
        qa_ref[:, h * HEAD_DIM:(h + 1) * HEAD_DIM] = (norm_rope(x, qg_ref[...]) * (scale * LOG2E)).astype(BF16)
    for h in range(A_KV_HEADS):
        x = a_ref[:, COL_AK + h * HEAD_DIM:COL_AK + (h + 1) * HEAD_DIM]
        ka_ref[:, h * HEAD_DIM:(h + 1) * HEAD_DIM] = norm_rope(x, kg_ref[...]).astype(BF16)
    va_ref[...] = a_ref[:, COL_AV:COL_AV + A_KV_W].astype(BF16)
    qc_ref[...] = (cq_ref[...] * scale).astype(BF16)
    kvc_ref[:, :C_W] = ck_ref[...].astype(BF16)
    kvc_ref[:, C_W:] = cv_ref[...].astype(BF16)


def prep_ac(proj, grp, cos, sin, q_gain, k_gain):
    tr = grp.row_tile(384)
    nb = grp.lp // tr
    base = grp.base // tr
    a_w = A_Q_W + 2 * A_KV_W

    def rowmap(col):
        return lambda b, i: (base + b * nb + i, col)

    out_map = lambda b, i: (b * nb + i, 0)
    n = grp.total
    return pl.pallas_call(
        _prep_ac_kernel,
        grid=(grp.bsz, nb),
        in_specs=[
            pl.BlockSpec((tr, a_w), rowmap(0)),
            pl.BlockSpec((tr, C_W), rowmap(COL_CQ // C_W)),
            pl.BlockSpec((tr, C_W), rowmap(COL_CK // C_W)),
            pl.BlockSpec((tr, C_W), rowmap(COL_CV // C_W)),
            pl.BlockSpec((tr, HEAD_DIM), lambda b, i: (i, 0)),
            pl.BlockSpec((tr, HEAD_DIM), lambda b, i: (i, 0)),
            pl.BlockSpec((1, HEAD_DIM), lambda b, i: (0, 0)),
            pl.BlockSpec((1, HEAD_DIM), lambda b, i: (0, 0)),
        ],
        out_specs=[
            pl.BlockSpec((tr, A_Q_W), out_map),
            pl.BlockSpec((tr, A_KV_W), out_map),
            pl.BlockSpec((tr, A_KV_W), out_map),
            pl.BlockSpec((tr, C_W), out_map),
            pl.BlockSpec((tr, 2 * C_W), out_map),
        ],
        out_shape=[
            jax.ShapeDtypeStruct((n, A_Q_W), BF16),
            jax.ShapeDtypeStruct((n, A_KV_W), BF16),
            jax.ShapeDtypeStruct((n, A_KV_W), BF16),
            jax.ShapeDtypeStruct((n, C_W), BF16),
            jax.ShapeDtypeStruct((n, 2 * C_W), BF16),
        ],
        compiler_params=_params(("parallel", "parallel")),
        name="prep_ac",
    )(proj, proj, proj, proj, cos, sin, q_gain.reshape(1, HEAD_DIM), k_gain.reshape(1, HEAD_DIM))


FLASH_CK = 256
FLASH_UNROLL = 4


def _flash_kernel(q_ref, k_ref, v_ref, bias_ref, mix_ref, o_ref, q2_ref, s_ref, m_ref, l_ref, acc_ref, *, tq, nchunks):
    q2_ref[:tq] = q_ref[:, :HEAD_DIM]
    q2_ref[tq:] = q_ref[:, HEAD_DIM:]
    q2 = q2_ref[...]

    def scores(t, slot):
        start = pl.multiple_of(t * FLASH_CK, FLASH_CK)
        k = k_ref[pl.ds(start, FLASH_CK), :]
        s_ref[slot] = lax.dot_general(q2, k, (((1,), (1,)), ((), ())), preferred_element_type=F32)

    def step(t, slot, masked, prefetch):
        if prefetch:
            scores(t + 1, 1 - slot)

        def read_scores():
            s = s_ref[slot]
            return s + bias_ref[pl.ds(t, 1), :] if masked else s

        m = m_ref[...]
        m_new = jnp.maximum(m, jnp.max(read_scores(), axis=-1, keepdims=True))
        m_ref[...] = m_new
        p = jnp.exp2(read_scores() - jnp.concatenate([m_new, m_new], axis=-1))
        alpha = jnp.exp2(m - m_new)
        v = v_ref[pl.ds(pl.multiple_of(t * FLASH_CK, FLASH_CK), FLASH_CK), :]
        pv = jnp.dot(p.astype(BF16), v, preferred_element_type=F32)
        l_ref[...] = alpha * l_ref[...] + pv[:, HEAD_DIM:]
        acc_ref[...] = alpha * acc_ref[...] + pv[:, :HEAD_DIM]

    m_ref[...] = jnp.full(m_ref.shape, -jnp.inf, F32)
    l_ref[...] = jnp.zeros(l_ref.shape, F32)
    acc_ref[...] = jnp.zeros(acc_ref.shape, F32)
    scores(0, 0)
    step(0, 0, True, nchunks > 1)
    if nchunks > 1:
        t0 = 1
        for _ in range((nchunks - 2) % FLASH_UNROLL):
            step(t0, t0 % 2, False, True)
            t0 += 1

        def group(j, carry):
            for u in range(FLASH_UNROLL):
                step(t0 + FLASH_UNROLL * j + u, (t0 + u) % 2, False, True)
            return carry

        lax.fori_loop(0, (nchunks - 1 - t0) // FLASH_UNROLL, group, 0)
        step(nchunks - 1, (nchunks - 1) % 2, True, False)
    o = acc_ref[...] / l_ref[...]
    o_ref[:, :HEAD_DIM] = o[:tq].astype(o_ref.dtype)
    o_ref[:, HEAD_DIM:] = o[tq:].astype(o_ref.dtype)


def flash_attention(qa, ka, va, key_bias, mix, grp):
    lp = grp.lp
    lkv = ka.shape[0] // grp.bsz
    assert lkv % FLASH_CK == 0 and lkv - grp.valid_end <= FLASH_CK and HEAD_PAD <= FLASH_CK
    tq = _pick_tile(lp, 384, 128)
    nchunks = lkv // FLASH_CK
    nq = lp // tq
    gw = 2 * HEAD_DIM
    return pl.pallas_call(
        functools.partial(_flash_kernel, tq=tq, nchunks=nchunks),
        grid=(grp.bsz, A_KV_HEADS, nq),
        in_specs=[
            pl.BlockSpec((tq, gw), lambda b, g, i: (b * nq + i, g)),
            pl.BlockSpec((lkv, HEAD_DIM), lambda b, g, i: (b, g)),
            pl.BlockSpec((lkv, 2 * HEAD_DIM), lambda b, g, i: (b, g)),
            pl.BlockSpec((nchunks, FLASH_CK), lambda b, g, i: (0, 0)),
            pl.BlockSpec(memory_space=pl.ANY),
        ],
        out_specs=pl.BlockSpec((tq, gw), lambda b, g, i: (grp.base // tq + b * nq + i, MIX_COL_A // gw + g)),
        out_shape=jax.ShapeDtypeStruct(mix.shape, mix.dtype),
        input_output_aliases={4: 0},
        scratch_shapes=[
            pltpu.VMEM((2 * tq, HEAD_DIM), BF16),
            pltpu.VMEM((2, 2 * tq, FLASH_CK), F32),
            pltpu.VMEM((2 * tq, HEAD_DIM), F32),
            pltpu.VMEM((2 * tq, HEAD_DIM), F32),
            pltpu.VMEM((2 * tq, HEAD_DIM), F32),
        ],
        compiler_params=_params(("parallel", "parallel", "parallel")),
        name="flash_attention",
    )(qa, ka, va, key_bias.reshape(nchunks, FLASH_CK), mix)


def values_with_ones(va):
    v = va.reshape(va.shape[0], A_KV_HEADS, HEAD_DIM)
    return jnp.concatenate([v, jnp.ones_like(v)], axis=-1).reshape(va.shape[0], 2 * A_KV_W)


def pad_keys(x, grp, value=0):
    lkv = -(-grp.lp // FLASH_CK) * FLASH_CK
    x = x.reshape(grp.bsz, grp.lp, -1)
    x = jnp.pad(x, ((0, 0), (0, lkv - grp.lp), (0, 0)), constant_values=value)
    return x.reshape(grp.bsz * lkv, -1)


NA_KEYS = BLK + WIN_ROWS * GRID_W
NA_META_VARIANT = WIN_ROWS


def _na_bias_table(rel_bias, meta_bias):
    rel_bias = rel_bias.astype(F32)
    qc = jnp.arange(GRID_W, dtype=jnp.int32)
    c = jnp.arange(GRID_W, dtype=jnp.int32)
    col_start = jnp.clip(qc - WIN_COLS // 2, 0, GRID_W - WIN_COLS)
    in_win = (c[None, :] >= col_start[:, None]) & (c[None, :] < col_start[:, None] + WIN_COLS)
    dc = jnp.clip(c[None, :] - qc[:, None] + (WIN_COLS - 1), 0, 2 * WIN_COLS - 2)
    full = jnp.where(in_win[None, None], rel_bias[:, :, dc], NEG)
    r = jnp.arange(WIN_ROWS, dtype=jnp.int32)
    variants = []
    for d0 in range(WIN_ROWS):
        band = full[:, d0 + r]
        variants.append(jnp.moveaxis(band, 1, 2).reshape(C_HEADS, GRID_W, WIN_ROWS * GRID_W))
    meta_band = jnp.broadcast_to(variants[WIN_ROWS - 1][:, :1], (C_HEADS, GRID_W, WIN_ROWS * GRID_W))
    variants.append(meta_band)
    band = jnp.stack(variants)
    mcol = jnp.concatenate([jnp.full((C_HEADS, HEAD_PAD), NEG, F32), meta_bias.astype(F32)], axis=-1)
    mcol = jnp.broadcast_to(mcol[None, :, None, :], (WIN_ROWS + 1, C_HEADS, GRID_W, BLK))
    return jnp.concatenate([mcol, band], axis=-1)


def _na_row_start(j, rows):
    return jnp.clip(j - 1 - WIN_ROWS // 2, 0, rows - WIN_ROWS)


NA_PAIR = 2


def _na_kernel(q_ref, meta_ref, *rest, rows):
    band_refs = rest[:NA_PAIR * WIN_ROWS]
    bias_ref, _, o_ref = rest[NA_PAIR * WIN_ROWS:]
    meta = meta_ref[...]
    items = []
    for u in range(NA_PAIR):
        j = jnp.minimum(pl.program_id(1) * NA_PAIR + u, rows)
        variant = jnp.where(j == 0, NA_META_VARIANT, _na_row_start(j, rows) - (j - 1) + (WIN_ROWS - 1))
        kv = jnp.concatenate([meta] + [r[...] for r in band_refs[u * WIN_ROWS:(u + 1) * WIN_ROWS]], axis=0)
        for h in range(C_HEADS):
            items.append((slice(u * BLK, (u + 1) * BLK), slice(h * HEAD_DIM, (h + 1) * HEAD_DIM), kv, variant, h))
    s = [lax.dot_general(q_ref[rs, sl], kv[:, sl], (((1,), (1,)), ((), ())), preferred_element_type=F32)
         + bias_ref[variant, h] for rs, sl, kv, variant, h in items]
    e = [jnp.exp(x - jnp.max(x, axis=-1, keepdims=True)) for x in s]
    p = [(x / jnp.sum(x, axis=-1, keepdims=True)).astype(BF16) for x in e]
    o = [jnp.dot(p[n], kv[:, C_W + sl.start:C_W + sl.stop], preferred_element_type=F32)
         for n, (rs, sl, kv, variant, h) in enumerate(items)]
    for n, (rs, sl, kv, variant, h) in enumerate(items):
        o_ref[rs, sl] = o[n].astype(o_ref.dtype)


def neighbourhood_attention(qc, kvc, bias_table, mix, grp):
    nblk = grp.lp // BLK
    rows = grp.rows
    assert nblk % NA_PAIR == 0
    npair = nblk // NA_PAIR
    qrows = NA_PAIR * BLK

    def band_map(u, r):
        def index_map(b, j):
            jj = jnp.minimum(j * NA_PAIR + u, rows)
            return (b * nblk + 1 + _na_row_start(jj, rows) + r, 0)
        return index_map

    in_specs = [
        pl.BlockSpec((qrows, C_W), lambda b, j: (b * npair + j, 0)),
        pl.BlockSpec((BLK, 2 * C_W), lambda b, j: (b * nblk, 0)),
    ]
    in_specs += [pl.BlockSpec((BLK, 2 * C_W), band_map(u, r)) for u in range(NA_PAIR) for r in range(WIN_ROWS)]
    in_specs += [pl.BlockSpec(bias_table.shape, lambda b, j: (0, 0, 0, 0)), pl.BlockSpec(memory_space=pl.ANY)]
    return pl.pallas_call(
        functools.partial(_na_kernel, rows=rows),
        grid=(grp.bsz, npair),
        in_specs=in_specs,
        out_specs=pl.BlockSpec((qrows, C_W), lambda b, j: (grp.base // qrows + b * npair + j, MIX_COL_C // C_W)),
        out_shape=jax.ShapeDtypeStruct(mix.shape, mix.dtype),
        input_output_aliases={len(in_specs) - 1: 0},
        compiler_params=_params(("parallel", "parallel")),
        name="neighbourhood_attention",
    )(qc, kvc, *([kvc] * (NA_PAIR * WIN_ROWS)), bias_table, mix)


HALO = 8
GATE_G = 0
GATE_BETA = 2 * B_HEADS


def _dn_prep_kernel(xp_ref, xc_ref, xn_ref, gate_ref, w_ref, a_ref, dtb_ref, qkv_ref, gb_ref, win_ref,
                    *, tb, valid_end):
    i = pl.program_id(1)
    part = pl.program_id(2)
    pos_w = i * tb - HALO + lax.broadcasted_iota(jnp.int32, (tb + 2 * HALO, 1), 0)
    ok_w = (pos_w >= HEAD_PAD) & (pos_w < valid_end)
    win_ref[0:HALO] = xp_ref[...]
    win_ref[HALO:HALO + tb] = xc_ref[...]
    win_ref[HALO + tb:] = xn_ref[...]
    win_ref[...] = jnp.where(ok_w, win_ref[...], 0.0)
    y = jnp.zeros((tb, B_W), F32)
    for j in range(CONV_K):
        y = y + win_ref[pl.ds(HALO - CONV_K // 2 + j, tb), :] * w_ref[j:j + 1, :]
    y = y * jax.nn.sigmoid(y)
    pos = i * tb + lax.broadcasted_iota(jnp.int32, (tb, 1), 0)
    ok = (pos >= HEAD_PAD) & (pos < valid_end)
    unit = jnp.where(part == 0, HEAD_DIM ** -0.5, 1.0)
    for h in range(B_HEADS):
        yh = y[:, h * HEAD_DIM:(h + 1) * HEAD_DIM]
        inv = lax.rsqrt(jnp.sum(yh * yh, axis=-1, keepdims=True) + EPS) * unit
        fac = jnp.where(part == 2, 1.0, inv)
        qkv_ref[:, h * HEAD_DIM:(h + 1) * HEAD_DIM] = jnp.where(ok, yh * fac, 0.0)

    @pl.when(part == 0)
    def _():
        x = gate_ref[...]
        lane = lax.broadcasted_iota(jnp.int32, x.shape, 1)
        z = x + dtb_ref[...]
        softplus = jnp.maximum(z, 0.0) + jnp.log1p(jnp.exp(-jnp.abs(z)))
        g = -jnp.exp(a_ref[...]) * softplus
        beta = jax.nn.sigmoid(x)
        out = jnp.where(lane < GATE_BETA, g, jnp.where(lane < 2 * GATE_BETA, beta, 0.0))
        gb_ref[...] = jnp.where(ok, out, 0.0)


def dn_prep(proj, grp, conv_w, a_log, dt_bias):
    tb = grp.row_tile(384)
    nb = grp.lp // tb
    base = grp.base // tb
    hb = tb // HALO
    last_halo = proj.shape[0] // HALO - 1
    qkv_col = COL_BQKV // B_W
    pad = jnp.zeros((1, 128 - 2 * B_HEADS), F32)
    a_row = jnp.concatenate([a_log.astype(F32).reshape(1, 2 * B_HEADS), pad], axis=-1)
    dtb_row = jnp.concatenate([dt_bias.astype(F32).reshape(1, 2 * B_HEADS), pad], axis=-1)
    return pl.pallas_call(
        functools.partial(_dn_prep_kernel, tb=tb, valid_end=grp.valid_end),
        grid=(grp.bsz, nb, 3),
        in_specs=[
            pl.BlockSpec((HALO, B_W), lambda b, i, p: (jnp.maximum((base + b * nb + i) * hb - 1, 0), qkv_col + p)),
            pl.BlockSpec((tb, B_W), lambda b, i, p: (base + b * nb + i, qkv_col + p)),
            pl.BlockSpec((HALO, B_W), lambda b, i, p: (jnp.minimum((base + b * nb + i + 1) * hb, last_halo), qkv_col + p)),
            pl.BlockSpec((tb, 128), lambda b, i, p: (base + b * nb + i, COL_GATE // 128)),
            pl.BlockSpec((CONV_K, B_W), lambda b, i, p: (0, p)),
            pl.BlockSpec((1, 128), lambda b, i, p: (0, 0)),
            pl.BlockSpec((1, 128), lambda b, i, p: (0, 0)),
        ],
        out_specs=[
            pl.BlockSpec((tb, B_W), lambda b, i, p: (b * nb + i, p)),
            pl.BlockSpec((tb, 128), lambda b, i, p: (b * nb + i, 0)),
        ],
        out_shape=[
            jax.ShapeDtypeStruct((grp.total, 3 * B_W), F32),
            jax.ShapeDtypeStruct((grp.total, 128), F32),
        ],
        scratch_shapes=[pltpu.VMEM((tb + 2 * HALO, B_W), F32)],
        compiler_params=_params(("parallel", "parallel", "arbitrary")),
        name="dn_prep",
    )(proj, proj, proj, proj, conv_w, a_row, dtb_row)


def _nt(a, b):
    return lax.dot_general(a, b, (((1,), (1,)), ((), ())), preferred_element_type=F32)


def _dn_scan_kernel(qf_ref, kf_ref, vf_ref, gf_ref, qb_ref, kb_ref, vb_ref, gb_ref, of_ref, ob_ref, s_ref):
    @pl.when(pl.program_id(1) == 0)
    def _():
        s_ref[...] = jnp.zeros_like(s_ref)

    ri = lax.broadcasted_iota(jnp.int32, (CHUNK, CHUNK), 0)
    ci = lax.broadcasted_iota(jnp.int32, (CHUNK, CHUNK), 1)
    eye = (ri == ci).astype(F32)
    eye128 = (lax.broadcasted_iota(jnp.int32, (128, 128), 0) == lax.broadcasted_iota(jnp.int32, (128, 128), 1)).astype(F32)
    dirs = ((qf_ref, kf_ref, vf_ref, gf_ref, of_ref), (qb_ref, kb_ref, vb_ref, gb_ref, ob_ref))
    n = 2 * B_HEADS
    incl, strict, gcc, gcr, g_tot, beta, refs = [], [], [], [], [], [], []
    for d, (q_ref, k_ref, v_ref, g_ref, o_ref) in enumerate(dirs):
        incl_d = (ri >= ci) if d == 0 else (ri <= ci)
        strict_d = (ri > ci) if d == 0 else (ri < ci)
        gates = g_ref[...]
        cum = jnp.dot(incl_d.astype(F32), gates, preferred_element_type=F32, precision=lax.Precision.HIGHEST)
        cum_t = lax.dot_general(eye128, cum, (((1,), (1,)), ((), ())), preferred_element_type=F32,
                                precision=lax.Precision.HIGHEST)
        last = CHUNK - 1 if d == 0 else 0
        for h in range(B_HEADS):
            c = GATE_G + d * B_HEADS + h
            sl = slice(h * HEAD_DIM, (h + 1) * HEAD_DIM)
            incl.append(incl_d)
            strict.append(strict_d)
            gcc.append(cum[:, c:c + 1])
            gcr.append(cum_t[c:c + 1, :])
            g_tot.append(cum_t[c:c + 1, last:last + 1])
            beta.append(gates[:, GATE_BETA + c:GATE_BETA + c + 1])
            refs.append((q_ref, k_ref, v_ref, o_ref, sl))

    decay = [jnp.exp(jnp.where(incl[i], gcc[i] - gcr[i], -jnp.inf)) for i in range(n)]
    k = [refs[i][1][:, refs[i][4]] for i in range(n)]
    kb = [k[i] * beta[i] for i in range(n)]
    k16 = [k[i].astype(BF16) for i in range(n)]
    q = [refs[i][0][:, refs[i][4]] for i in range(n)]
    kk_qk = [_nt(jnp.concatenate([kb[i], q[i]], axis=0).astype(BF16), k16[i]) for i in range(n)]
    x = [-jnp.where(strict[i], kk_qk[i][:CHUNK] * decay[i], 0.0) for i in range(n)]
    qk = [jnp.where(incl[i], kk_qk[i][CHUNK:] * decay[i], 0.0).astype(BF16) for i in range(n)]
    t = [x[i] + eye for i in range(n)]
    x16 = [x[i].astype(BF16) for i in range(n)]
    p = [jnp.dot(x16[i], x16[i], preferred_element_type=F32) for i in range(n)]
    for _ in range(4):
        p16 = [p[i].astype(BF16) for i in range(n)]
        tp = [jnp.dot(jnp.concatenate([t[i].astype(BF16), p16[i]], axis=0), p16[i], preferred_element_type=F32)
              for i in range(n)]
        t = [t[i] + tp[i][:CHUNK] for i in range(n)]
        p = [tp[i][CHUNK:] for i in range(n)]
    t = [t[i] + jnp.dot(t[i].astype(BF16), p[i].astype(BF16), preferred_element_type=F32) for i in range(n)]
    egc = [jnp.exp(gcc[i]) for i in range(n)]
    rhs = [jnp.concatenate([kb[i] * egc[i], refs[i][2][:, refs[i][4]] * beta[i]], axis=-1).astype(BF16) for i in range(n)]
    wu = [jnp.dot(t[i].astype(BF16), rhs[i], preferred_element_type=F32) for i in range(n)]
    s = [s_ref[i] for i in range(n)]
    lhs = [jnp.concatenate([wu[i][:, :HEAD_DIM], q[i] * egc[i]], axis=0).astype(BF16) for i in range(n)]
    both = [jnp.dot(lhs[i], s[i].astype(BF16), preferred_element_type=F32) for i in range(n)]
    v16 = [(wu[i][:, HEAD_DIM:] - both[i][:CHUNK]).astype(BF16) for i in range(n)]
    for i in range(n):
        o_ref, sl = refs[i][3], refs[i][4]
        o_ref[:, sl] = both[i][CHUNK:] + jnp.dot(qk[i], v16[i], preferred_element_type=F32)
    kd = [(k[i] * jnp.exp(g_tot[i] - gcc[i])).astype(BF16) for i in range(n)]
    for i in range(n):
        s_ref[i] = s[i] * jnp.exp(g_tot[i]) + lax.dot_general(
            kd[i], v16[i], (((0,), (0,)), ((), ())), preferred_element_type=F32)


def dn_scan(qkv, gb, grp):
    nc = grp.lp // BLK

    def fw(col):
        return lambda b, c: (b * nc + c, col)

    def bw(col):
        return lambda b, c: (b * nc + nc - 1 - c, col)

    blk = (BLK, B_W)
    return pl.pallas_call(
        _dn_scan_kernel,
        grid=(grp.bsz, nc),
        in_specs=[
            pl.BlockSpec(blk, fw(0)), pl.BlockSpec(blk, fw(1)), pl.BlockSpec(blk, fw(2)), pl.BlockSpec((BLK, 128), fw(0)),
            pl.BlockSpec(blk, bw(0)), pl.BlockSpec(blk, bw(1)), pl.BlockSpec(blk, bw(2)), pl.BlockSpec((BLK, 128), bw(0)),
        ],
        out_specs=[pl.BlockSpec(blk, fw(0)), pl.BlockSpec(blk, bw(0))],
        out_shape=[jax.ShapeDtypeStruct((grp.total, B_W), F32)] * 2,
        scratch_shapes=[pltpu.VMEM((2 * B_HEADS, HEAD_DIM, HEAD_DIM), F32)],
        compiler_params=_params(("parallel", "arbitrary")),
        name="dn_scan",
    )(qkv, qkv, qkv, gb, qkv, qkv, qkv, gb)


def _dn_post_kernel(of_ref, ob_ref, z_ref, gain_ref, mix_ref, o_ref):
    for h in range(B_HEADS):
        lo, hi = h * HEAD_DIM, (h + 1) * HEAD_DIM
        o = of_ref[:, lo:hi] + ob_ref[:, lo:hi]
        ms = jnp.mean(o * o, axis=-1, keepdims=True)
        z = z_ref[:, lo:hi]
        o_ref[:, lo:hi] = (o * lax.rsqrt(ms + EPS) * gain_ref[...] * (z * jax.nn.sigmoid(z))).astype(o_ref.dtype)


def dn_post(o_fw, o_bw, proj, mix, grp, out_gain):
    tb = grp.row_tile(384)
    nb = grp.lp // tb
    base = grp.base // tb
    return pl.pallas_call(
        _dn_post_kernel,
        grid=(grp.bsz * nb,),
        in_specs=[
            pl.BlockSpec((tb, B_W), lambda i: (i, 0)),
            pl.BlockSpec((tb, B_W), lambda i: (i, 0)),
            pl.BlockSpec((tb, B_W), lambda i: (base + i, COL_BZ // B_W)),
            pl.BlockSpec((1, HEAD_DIM), lambda i: (0, 0)),
            pl.BlockSpec(memory_space=pl.ANY),
        ],
        out_specs=pl.BlockSpec((tb, B_W), lambda i: (base + i, MIX_COL_B // B_W)),
        out_shape=jax.ShapeDtypeStruct(mix.shape, mix.dtype),
        input_output_aliases={4: 0},
        compiler_params=_params(("parallel",)),
        name="dn_post",
    )(o_fw, o_bw, proj, out_gain.reshape(1, HEAD_DIM), mix)


def _permute_w_in(w_in):
    sizes = (A_Q_W, A_KV_W, A_KV_W, 3 * B_W, B_W, 2 * B_HEADS, 2 * B_HEADS, C_W, C_W, C_W)
    offs = [0]
    for s in sizes:
        offs.append(offs[-1] + s)
    w_in = w_in.astype(BF16)
    seg = [w_in[..., offs[n]:offs[n + 1]] for n in range(len(sizes))]
    aq, ak, av, bqkv, bz, ba, bb, cq, ck, cv = seg
    used = COL_GATE + 4 * B_HEADS
    pad = jnp.zeros(w_in.shape[:-1] + (IN_WIDTH_PAD - used,), w_in.dtype)
    return jnp.concatenate([aq, ak, av, bqkv, bz, cq, ck, cv, ba, bb, pad], axis=-1)


def _stack_tokens(xs, groups, meta_tokens):
    parts = []
    for x, grp in zip(xs, groups):
        d = x.shape[-1]
        front_meta = jnp.concatenate([jnp.zeros((HEAD_PAD, d), x.dtype), meta_tokens.astype(x.dtype)], axis=0)
        tail = jnp.zeros((grp.lp - grp.valid_end, d), x.dtype)
        for b in range(grp.bsz):
            parts += [front_meta, x[b], tail]
    return jnp.concatenate(parts, axis=0)


def kernel(x_prompt, x_sample, meta_tokens, ffn1_norm, ffn1_w_gate, ffn1_w_up, ffn1_w_down, mix_norm, w_in,
           attn_q_norm, attn_k_norm, dn_conv_w, dn_a_log, dn_dt_bias, dn_out_norm, na_rel_bias, na_meta_bias,
           w_out, ffn2_norm, ffn2_w_gate, ffn2_w_up, ffn2_w_down, final_norm):
    depth = w_in.shape[0]
    g_prompt = Group(x_prompt.shape[0], x_prompt.shape[1], 0)
    g_sample = Group(x_sample.shape[0], x_sample.shape[1], g_prompt.total)
    groups = (g_prompt, g_sample)
    x = _stack_tokens((x_prompt, x_sample), groups, meta_tokens)
    tables = [_position_tables(grp) for grp in groups]

    w1g, w1u, w1d = (w.astype(BF16) for w in (ffn1_w_gate, ffn1_w_up, ffn1_w_down))
    w2g, w2u, w2d = (w.astype(BF16) for w in (ffn2_w_gate, ffn2_w_up, ffn2_w_down))
    w_in_p = _permute_w_in(w_in)
    w_out_b = jnp.concatenate([w_out[:, A_Q_W:A_Q_W + B_W], w_out[:, :A_Q_W], w_out[:, A_Q_W + B_W:]], axis=1).astype(BF16)
    mix = jnp.zeros((x.shape[0], MIX_WIDTH), BF16)

    for l in range(depth):
        x = residual_matmul(x, norm_gate_up(x, ffn1_norm[l], w1g, w1u, l), w1d, l, 0.5)

        proj = norm_matmul(x, mix_norm[l], w_in_p, l, IN_TILE, F32)
        bias_table = _na_bias_table(na_rel_bias[l], na_meta_bias[l])
        for grp, (cos, sin, key_bias) in zip(groups, tables):
            qa, ka, va, qc, kvc = prep_ac(proj, grp, cos, sin, attn_q_norm[l], attn_k_norm[l])
            mix = flash_attention(qa, pad_keys(ka, grp), values_with_ones(pad_keys(va, grp)), key_bias, mix, grp)
            mix = neighbourhood_attention(qc, kvc, bias_table, mix, grp)
            qkv, gb = dn_prep(proj, grp, dn_conv_w[l], dn_a_log[l], dn_dt_bias[l])
            o_fw, o_bw = dn_scan(qkv, gb, grp)
            mix = dn_post(o_fw, o_bw, proj, mix, grp, dn_out_norm[l])
        x = residual_matmul(x, mix, w_out_b, l, 1.0)

        x = residual_matmul(x, norm_gate_up(x, ffn2_norm[l], w2g, w2u, l), w2d, l, 0.5)

    return tuple(final_rmsnorm(x, final_norm, grp) for grp in groups)
```

```python
import functools

import jax
import jax.numpy as jnp
from jax import lax
from jax.experimental import pallas as pl
from jax.experimental.pallas import tpu as pltpu

F32 = jnp.float32
BF16 = jnp.bfloat16

HEAD_DIM = 128
N_META = 16
GRID_W = 64
EPS = 1e-6
A_HEADS = 4
A_KV_HEADS = 2
ROPE_THETA = 10000.0
B_HEADS = 8
CONV_K = 5
CHUNK = 64
C_HEADS = 4
WIN_ROWS = 8
WIN_COLS = 16

A_Q_W = A_HEADS * HEAD_DIM
A_KV_W = A_KV_HEADS * HEAD_DIM
B_W = B_HEADS * HEAD_DIM
C_W = C_HEADS * HEAD_DIM
MIX_WIDTH = A_Q_W + B_W + C_W

BLK = 64
HEAD_PAD = BLK - N_META
SEQ_ALIGN = 128
NEG = -1e30
LOG2E = 1.4426950408889634

COL_AQ = 0
COL_AK = COL_AQ + A_Q_W
COL_AV = COL_AK + A_KV_W
COL_BQKV = COL_AV + A_KV_W
COL_BZ = COL_BQKV + 3 * B_W
COL_CQ = COL_BZ + B_W
COL_CK = COL_CQ + C_W
COL_CV = COL_CK + C_W
COL_GATE = COL_CV + C_W
MIX_COL_B = 0
MIX_COL_A = MIX_COL_B + B_W
MIX_COL_C = MIX_COL_A + A_Q_W
IN_TILE = 768
IN_WIDTH_PAD = -(-(COL_GATE + 128) // IN_TILE) * IN_TILE

V7X_VMEM_BYTES = 64 * 1024 * 1024
VMEM_LIMIT = 52 * 1024 * 1024


def _params(sem):
    return pltpu.CompilerParams(dimension_semantics=sem, vmem_limit_bytes=VMEM_LIMIT)


def _pick_tile(n, target, mult):
    best = None
    for t in range(mult, min(n, target) + 1, mult):
        if n % t == 0:
            best = t
    assert best is not None, (n, target, mult)
    return best


def _rmsnorm_kernel(x_ref, g_ref, o_ref):
    x = x_ref[...]
    ms = jnp.mean(x * x, axis=-1, keepdims=True)
    o_ref[...] = (x * lax.rsqrt(ms + EPS) * g_ref[...]).astype(o_ref.dtype)


NORM_SUB_ROWS = 16
NORM_UNROLL = 4


def _norm_rows(x_ref, g_ref, h_ref):
    @pl.when(pl.program_id(1) == 0)
    def _():
        rows = x_ref.shape[0]
        sub = NORM_SUB_ROWS
        unroll = max(u for u in range(1, NORM_UNROLL + 1) if rows % (sub * u) == 0)

        def body(r, carry):
            for u in range(unroll):
                rs = pl.ds(pl.multiple_of((r * unroll + u) * sub, sub), sub)
                x = x_ref[rs, :]
                ms = jnp.mean(x * x, axis=-1, keepdims=True)
                h_ref[rs, :] = (x * lax.rsqrt(ms + EPS) * g_ref[...]).astype(h_ref.dtype)
            return carry

        lax.fori_loop(0, rows // (sub * unroll), body, 0)


def _norm_gate_up_kernel(x_ref, g_ref, wg_ref, wu_ref, o_ref, h_ref):
    _norm_rows(x_ref, g_ref, h_ref)
    h = h_ref[...]
    g = jnp.dot(h, wg_ref[...], preferred_element_type=F32)
    u = jnp.dot(h, wu_ref[...], preferred_element_type=F32)
    o_ref[...] = (g * jax.nn.sigmoid(g) * u).astype(o_ref.dtype)


def norm_gate_up(x, gain, wg, wu, layer):
    t, d = x.shape
    f = wg.shape[-1]
    tm = _pick_tile(t, 640, 16)
    tn = _pick_tile(f, 512, 128)
    return pl.pallas_call(
        _norm_gate_up_kernel,
        grid=(t // tm, f // tn),
        in_specs=[
            pl.BlockSpec((tm, d), lambda i, j: (i, 0)),
            pl.BlockSpec((1, d), lambda i, j: (0, 0)),
            pl.BlockSpec((None, d, tn), lambda i, j: (layer, 0, j)),
            pl.BlockSpec((None, d, tn), lambda i, j: (layer, 0, j)),
        ],
        out_specs=pl.BlockSpec((tm, tn), lambda i, j: (i, j)),
        out_shape=jax.ShapeDtypeStruct((t, f), BF16),
        scratch_shapes=[pltpu.VMEM((tm, d), BF16)],
        compiler_params=_params(("parallel", "arbitrary")),
        name="norm_gate_up",
    )(x, gain.reshape(1, d), wg, wu)


def _norm_matmul_kernel(x_ref, g_ref, w_ref, o_ref, h_ref):
    _norm_rows(x_ref, g_ref, h_ref)
    o_ref[...] = jnp.dot(h_ref[...], w_ref[...], preferred_element_type=F32).astype(o_ref.dtype)


def norm_matmul(x, gain, w, layer, tn_target, out_dtype):
    t, d = x.shape
    n = w.shape[-1]
    tm = _pick_tile(t, 640, 16)
    tn = _pick_tile(n, tn_target, 128)
    return pl.pallas_call(
        _norm_matmul_kernel,
        grid=(t // tm, n // tn),
        in_specs=[
            pl.BlockSpec((tm, d), lambda i, j: (i, 0)),
            pl.BlockSpec((1, d), lambda i, j: (0, 0)),
            pl.BlockSpec((None, d, tn), lambda i, j: (layer, 0, j)),
        ],
        out_specs=pl.BlockSpec((tm, tn), lambda i, j: (i, j)),
        out_shape=jax.ShapeDtypeStruct((t, n), out_dtype),
        scratch_shapes=[pltpu.VMEM((tm, d), BF16)],
        compiler_params=_params(("parallel", "arbitrary")),
        name="norm_matmul",
    )(x, gain.reshape(1, d), w)


def _residual_matmul_kernel(x_ref, a_ref, w_ref, o_ref, *, scale):
    o_ref[...] = x_ref[...] + scale * jnp.dot(a_ref[...], w_ref[...], preferred_element_type=F32)


def residual_matmul(x, a, w, layer, scale):
    t, d = x.shape
    k = a.shape[-1]
    tm = _pick_tile(t, 640, 16)
    tn = _pick_tile(d, 1024, 128)
    return pl.pallas_call(
        functools.partial(_residual_matmul_kernel, scale=scale),
        grid=(d // tn, t // tm),
        in_specs=[
            pl.BlockSpec((tm, tn), lambda j, i: (i, j)),
            pl.BlockSpec((tm, k), lambda j, i: (i, 0)),
            pl.BlockSpec((None, k, tn), lambda j, i: (layer, 0, j)),
        ],
        out_specs=pl.BlockSpec((tm, tn), lambda j, i: (i, j)),
        out_shape=jax.ShapeDtypeStruct((t, d), F32),
        compiler_params=_params(("parallel", "parallel")),
        name="residual_matmul",
    )(x, a, w)


def final_rmsnorm(x, gain, grp):
    d = x.shape[-1]
    nblk = grp.lp // BLK
    base = grp.base // BLK
    rows = grp.n_tok // BLK
    return pl.pallas_call(
        _rmsnorm_kernel,
        grid=(grp.bsz, rows),
        in_specs=[
            pl.BlockSpec((BLK, d), lambda b, i: (base + b * nblk + 1 + i, 0)),
            pl.BlockSpec((1, d), lambda b, i: (0, 0)),
        ],
        out_specs=pl.BlockSpec((None, BLK, d), lambda b, i: (b, i, 0)),
        out_shape=jax.ShapeDtypeStruct((grp.bsz, grp.n_tok, d), F32),
        compiler_params=_params(("parallel", "parallel")),
        name="final_norm",
    )(x, gain.reshape(1, d))


class Group:
    def __init__(self, bsz, n_tok, base):
        assert n_tok % GRID_W == 0 and n_tok // GRID_W >= WIN_ROWS
        self.bsz = bsz
        self.n_tok = n_tok
        self.lp = -(-(BLK + n_tok) // SEQ_ALIGN) * SEQ_ALIGN
        self.base = base
        self.rows = n_tok // GRID_W
        self.valid_end = BLK + n_tok

    @property
    def total(self):
        return self.bsz * self.lp

    def row_tile(self, target):
        best = SEQ_ALIGN
        for t in range(SEQ_ALIGN, target + 1, SEQ_ALIGN):
            if self.lp % t == 0 and self.base % t == 0:
                best = t
        return best


def _position_tables(grp):
    p = jnp.arange(grp.lp, dtype=jnp.int32)
    t = jnp.clip(p - BLK, 0, grp.n_tok - 1)
    is_tok = (p >= BLK) & (p < grp.valid_end)
    row = jnp.where(is_tok, t // GRID_W, 0).astype(F32)
    col = jnp.where(is_tok, t % GRID_W, 0).astype(F32)
    half = HEAD_DIM // 4
    freqs = ROPE_THETA ** (-jnp.arange(half, dtype=F32) / half)
    ang_r = row[:, None] * freqs[None, :]
    ang_c = col[:, None] * freqs[None, :]
    cos = jnp.concatenate([jnp.cos(ang_r), jnp.cos(ang_r), jnp.cos(ang_c), jnp.cos(ang_c)], axis=-1)
    sin = jnp.concatenate([-jnp.sin(ang_r), jnp.sin(ang_r), -jnp.sin(ang_c), jnp.sin(ang_c)], axis=-1)
    pk = jnp.arange(-(-grp.lp // FLASH_CK) * FLASH_CK, dtype=jnp.int32)
    key_bias = jnp.where((pk >= HEAD_PAD) & (pk < grp.valid_end), 0.0, NEG).astype(F32)
    return cos, sin, key_bias


def _swap_halves(x):
    lane = lax.broadcasted_iota(jnp.int32, x.shape, 1)
    return jnp.where(lane % 64 < 32, pltpu.roll(x, 96, 1), pltpu.roll(x, 32, 1))


def _prep_ac_kernel(a_ref, cq_ref, ck_ref, cv_ref, cos_ref, sin_ref, qg_ref, kg_ref,
                    qa_ref, ka_ref, va_ref, qc_ref, kvc_ref):
    cos = cos_ref[...]
    sin = sin_ref[...]
    scale = HEAD_DIM ** -0.5

    def norm_rope(x, gain):
        ms = jnp.mean(x * x, axis=-1, keepdims=True)
        y = x * lax.rsqrt(ms + EPS) * gain
        return y * cos + _swap_halves(y) * sin

    for h in range(A_HEADS):
        x = a_ref[:, COL_AQ + h * HEAD_DIM:COL_AQ + (h + 1) * HEAD_DIM]
        qa_ref[:, h * HEAD_DIM:(h + 1) * HEAD_DIM] = (norm_rope(x, qg_ref[...]) * (scale * LOG2E)).astype(BF16)
    for h in range(A_KV_HEADS):
        x = a_ref[:, COL_AK + h * HEAD_DIM:COL_AK + (h + 1) * HEAD_DIM]
        ka_ref[:, h * HEAD_DIM:(h + 1) * HEAD_DIM] = norm_rope(x, kg_ref[...]).astype(BF16)
    va_ref[...] = a_ref[:, COL_AV:COL_AV + A_KV_W].astype(BF16)
    qc_ref[...] = (cq_ref[...] * scale).astype(BF16)
    kvc_ref[:, :C_W] = ck_ref[...].astype(BF16)
    kvc_ref[:, C_W:] = cv_ref[...].astype(BF16)


def prep_ac(proj, grp, cos, sin, q_gain, k_gain):
    tr = grp.row_tile(384)
    nb = grp.lp // tr
    base = grp.base // tr
    a_w = A_Q_W + 2 * A_KV_W

    def rowmap(col):
        return lambda b, i: (base + b * nb + i, col)

    out_map = lambda b, i: (b * nb + i, 0)
    n = grp.total
    return pl.pallas_call(
        _prep_ac_kernel,
        grid=(grp.bsz, nb),
        in_specs=[
            pl.BlockSpec((tr, a_w), rowmap(0)),
            pl.BlockSpec((tr, C_W), rowmap(COL_CQ // C_W)),
            pl.BlockSpec((tr, C_W), rowmap(COL_CK // C_W)),
            pl.BlockSpec((tr, C_W), rowmap(COL_CV // C_W)),
            pl.BlockSpec((tr, HEAD_DIM), lambda b, i: (i, 0)),
            pl.BlockSpec((tr, HEAD_DIM), lambda b, i: (i, 0)),
            pl.BlockSpec((1, HEAD_DIM), lambda b, i: (0, 0)),
            pl.BlockSpec((1, HEAD_DIM), lambda b, i: (0, 0)),
        ],
        out_specs=[
            pl.BlockSpec((tr, A_Q_W), out_map),
            pl.BlockSpec((tr, A_KV_W), out_map),
            pl.BlockSpec((tr, A_KV_W), out_map),
            pl.BlockSpec((tr, C_W), out_map),
            pl.BlockSpec((tr, 2 * C_W), out_map),
        ],
        out_shape=[
            jax.ShapeDtypeStruct((n, A_Q_W), BF16),
            jax.ShapeDtypeStruct((n, A_KV_W), BF16),
            jax.ShapeDtypeStruct((n, A_KV_W), BF16),
            jax.ShapeDtypeStruct((n, C_W), BF16),
            jax.ShapeDtypeStruct((n, 2 * C_W), BF16),
        ],
        compiler_params=_params(("parallel", "parallel")),
        name="prep_ac",
    )(proj, proj, proj, proj, cos, sin, q_gain.reshape(1, HEAD_DIM), k_gain.reshape(1, HEAD_DIM))


FLASH_CK = 256
FLASH_UNROLL = 6


def _flash_kernel(q_ref, k_ref, v_ref, bias_ref, mix_ref, o_ref, q2_ref, s_ref, m_ref, l_ref, acc_ref, *, tq, nchunks):
    q2_ref[:tq] = q_ref[:, :HEAD_DIM]
    q2_ref[tq:] = q_ref[:, HEAD_DIM:]
    q2 = q2_ref[...]

    def scores(t, slot):
        start = pl.multiple_of(t * FLASH_CK, FLASH_CK)
        k = k_ref[pl.ds(start, FLASH_CK), :]
        s_ref[slot] = lax.dot_general(q2, k, (((1,), (1,)), ((), ())), preferred_element_type=F32)

    def step(t, slot, masked, prefetch):
        if prefetch:
            scores(t + 1, 1 - slot)

        def read_scores():
            s = s_ref[slot]
            return s + bias_ref[pl.ds(t, 1), :] if masked else s

        m = m_ref[...]
        m_new = jnp.maximum(m, jnp.max(read_scores(), axis=-1, keepdims=True))
        m_ref[...] = m_new
        p = jnp.exp2(read_scores() - jnp.concatenate([m_new, m_new], axis=-1))
        alpha = jnp.exp2(m - m_new)
        v = v_ref[pl.ds(pl.multiple_of(t * FLASH_CK, FLASH_CK), FLASH_CK), :]
        pv = jnp.dot(p.astype(BF16), v, preferred_element_type=F32)
        l_ref[...] = alpha * l_ref[...] + pv[:, HEAD_DIM:]
        acc_ref[...] = alpha * acc_ref[...] + pv[:, :HEAD_DIM]

    m_ref[...] = jnp.full(m_ref.shape, -jnp.inf, F32)
    l_ref[...] = jnp.zeros(l_ref.shape, F32)
    acc_ref[...] = jnp.zeros(acc_ref.shape, F32)
    scores(0, 0)
    step(0, 0, True, nchunks > 1)
    if nchunks > 1:
        t0 = 1
        for _ in range((nchunks - 2) % FLASH_UNROLL):
            step(t0, t0 % 2, False, True)
            t0 += 1

        def group(j, carry):
            for u in range(FLASH_UNROLL):
                step(t0 + FLASH_UNROLL * j + u, (t0 + u) % 2, False, True)
            return carry

        lax.fori_loop(0, (nchunks - 1 - t0) // FLASH_UNROLL, group, 0)
        step(nchunks - 1, (nchunks - 1) % 2, True, False)
    o = acc_ref[...] / l_ref[...]
    o_ref[:, :HEAD_DIM] = o[:tq].astype(o_ref.dtype)
    o_ref[:, HEAD_DIM:] = o[tq:].astype(o_ref.dtype)


def flash_attention(qa, ka, va, key_bias, mix, grp):
    lp = grp.lp
    lkv = ka.shape[0] // grp.bsz
    assert lkv % FLASH_CK == 0 and lkv - grp.valid_end <= FLASH_CK and HEAD_PAD <= FLASH_CK
    tq = _pick_tile(lp, 384, 128)
    nchunks = lkv // FLASH_CK
    nq = lp // tq
    gw = 2 * HEAD_DIM
    return pl.pallas_call(
        functools.partial(_flash_kernel, tq=tq, nchunks=nchunks),
        grid=(grp.bsz, A_KV_HEADS, nq),
        in_specs=[
            pl.BlockSpec((tq, gw), lambda b, g, i: (b * nq + i, g)),
            pl.BlockSpec((lkv, HEAD_DIM), lambda b, g, i: (b, g)),
            pl.BlockSpec((lkv, 2 * HEAD_DIM), lambda b, g, i: (b, g)),
            pl.BlockSpec((nchunks, FLASH_CK), lambda b, g, i: (0, 0)),
            pl.BlockSpec(memory_space=pl.ANY),
        ],
        out_specs=pl.BlockSpec((tq, gw), lambda b, g, i: (grp.base // tq + b * nq + i, MIX_COL_A // gw + g)),
        out_shape=jax.ShapeDtypeStruct(mix.shape, mix.dtype),
        input_output_aliases={4: 0},
        scratch_shapes=[
            pltpu.VMEM((2 * tq, HEAD_DIM), BF16),
            pltpu.VMEM((2, 2 * tq, FLASH_CK), F32),
            pltpu.VMEM((2 * tq, HEAD_DIM), F32),
            pltpu.VMEM((2 * tq, HEAD_DIM), F32),
            pltpu.VMEM((2 * tq, HEAD_DIM), F32),
        ],
        compiler_params=_params(("parallel", "parallel", "parallel")),
        name="flash_attention",
    )(qa, ka, va, key_bias.reshape(nchunks, FLASH_CK), mix)


def values_with_ones(va):
    v = va.reshape(va.shape[0], A_KV_HEADS, HEAD_DIM)
    return jnp.concatenate([v, jnp.ones_like(v)], axis=-1).reshape(va.shape[0], 2 * A_KV_W)


def pad_keys(x, grp, value=0):
    lkv = -(-grp.lp // FLASH_CK) * FLASH_CK
    x = x.reshape(grp.bsz, grp.lp, -1)
    x = jnp.pad(x, ((0, 0), (0, lkv - grp.lp), (0, 0)), constant_values=value)
    return x.reshape(grp.bsz * lkv, -1)


NA_KEYS = BLK + WIN_ROWS * GRID_W
NA_META_VARIANT = WIN_ROWS


def _na_bias_table(rel_bias, meta_bias):
    rel_bias = rel_bias.astype(F32)
    qc = jnp.arange(GRID_W, dtype=jnp.int32)
    c = jnp.arange(GRID_W, dtype=jnp.int32)
    col_start = jnp.clip(qc - WIN_COLS // 2, 0, GRID_W - WIN_COLS)
    in_win = (c[None, :] >= col_start[:, None]) & (c[None, :] < col_start[:, None] + WIN_COLS)
    dc = jnp.clip(c[None, :] - qc[:, None] + (WIN_COLS - 1), 0, 2 * WIN_COLS - 2)
    full = jnp.where(in_win[None, None], rel_bias[:, :, dc], NEG)
    r = jnp.arange(WIN_ROWS, dtype=jnp.int32)
    variants = []
    for d0 in range(WIN_ROWS):
        band = full[:, d0 + r]
        variants.append(jnp.moveaxis(band, 1, 2).reshape(C_HEADS, GRID_W, WIN_ROWS * GRID_W))
    meta_band = jnp.broadcast_to(variants[WIN_ROWS - 1][:, :1], (C_HEADS, GRID_W, WIN_ROWS * GRID_W))
    variants.append(meta_band)
    band = jnp.stack(variants)
    mcol = jnp.concatenate([jnp.full((C_HEADS, HEAD_PAD), NEG, F32), meta_bias.astype(F32)], axis=-1)
    mcol = jnp.broadcast_to(mcol[None, :, None, :], (WIN_ROWS + 1, C_HEADS, GRID_W, BLK))
    return jnp.concatenate([mcol, band], axis=-1)


def _na_row_start(j, rows):
    return jnp.clip(j - 1 - WIN_ROWS // 2, 0, rows - WIN_ROWS)


NA_PAIR = 2


def _na_kernel(q_ref, meta_ref, *rest, rows):
    band_refs = rest[:NA_PAIR * WIN_ROWS]
    bias_ref, _, o_ref = rest[NA_PAIR * WIN_ROWS:]
    meta = meta_ref[...]
    items = []
    for u in range(NA_PAIR):
        j = jnp.minimum(pl.program_id(1) * NA_PAIR + u, rows)
        variant = jnp.where(j == 0, NA_META_VARIANT, _na_row_start(j, rows) - (j - 1) + (WIN_ROWS - 1))
        kv = jnp.concatenate([meta] + [r[...] for r in band_refs[u * WIN_ROWS:(u + 1) * WIN_ROWS]], axis=0)
        for h in range(C_HEADS):
            items.append((slice(u * BLK, (u + 1) * BLK), slice(h * HEAD_DIM, (h + 1) * HEAD_DIM), kv, variant, h))
    s = [lax.dot_general(q_ref[rs, sl], kv[:, sl], (((1,), (1,)), ((), ())), preferred_element_type=F32)
         + bias_ref[variant, h] for rs, sl, kv, variant, h in items]
    e = [jnp.exp(x - jnp.max(x, axis=-1, keepdims=True)) for x in s]
    p = [(x / jnp.sum(x, axis=-1, keepdims=True)).astype(BF16) for x in e]
    o = [jnp.dot(p[n], kv[:, C_W + sl.start:C_W + sl.stop], preferred_element_type=F32)
         for n, (rs, sl, kv, variant, h) in enumerate(items)]
    for n, (rs, sl, kv, variant, h) in enumerate(items):
        o_ref[rs, sl] = o[n].astype(o_ref.dtype)


def neighbourhood_attention(qc, kvc, bias_table, mix, grp):
    nblk = grp.lp // BLK
    rows = grp.rows
    assert nblk % NA_PAIR == 0
    npair = nblk // NA_PAIR
    qrows = NA_PAIR * BLK

    def band_map(u, r):
        def index_map(b, j):
            jj = jnp.minimum(j * NA_PAIR + u, rows)
            return (b * nblk + 1 + _na_row_start(jj, rows) + r, 0)
        return index_map

    in_specs = [
        pl.BlockSpec((qrows, C_W), lambda b, j: (b * npair + j, 0)),
        pl.BlockSpec((BLK, 2 * C_W), lambda b, j: (b * nblk, 0)),
    ]
    in_specs += [pl.BlockSpec((BLK, 2 * C_W), band_map(u, r)) for u in range(NA_PAIR) for r in range(WIN_ROWS)]
    in_specs += [pl.BlockSpec(bias_table.shape, lambda b, j: (0, 0, 0, 0)), pl.BlockSpec(memory_space=pl.ANY)]
    return pl.pallas_call(
        functools.partial(_na_kernel, rows=rows),
        grid=(grp.bsz, npair),
        in_specs=in_specs,
        out_specs=pl.BlockSpec((qrows, C_W), lambda b, j: (grp.base // qrows + b * npair + j, MIX_COL_C // C_W)),
        out_shape=jax.ShapeDtypeStruct(mix.shape, mix.dtype),
        input_output_aliases={len(in_specs) - 1: 0},
        compiler_params=_params(("parallel", "parallel")),
        name="neighbourhood_attention",
    )(qc, kvc, *([kvc] * (NA_PAIR * WIN_ROWS)), bias_table, mix)


HALO = 8
GATE_G = 0
GATE_BETA = 2 * B_HEADS


def _dn_prep_kernel(xp_ref, xc_ref, xn_ref, gate_ref, w_ref, a_ref, dtb_ref, qkv_ref, gb_ref, win_ref,
                    *, tb, valid_end):
    i = pl.program_id(1)
    part = pl.program_id(2)
    pos_w = i * tb - HALO + lax.broadcasted_iota(jnp.int32, (tb + 2 * HALO, 1), 0)
    ok_w = (pos_w >= HEAD_PAD) & (pos_w < valid_end)
    win_ref[0:HALO] = xp_ref[...]
    win_ref[HALO:HALO + tb] = xc_ref[...]
    win_ref[HALO + tb:] = xn_ref[...]
    win_ref[...] = jnp.where(ok_w, win_ref[...], 0.0)
    y = jnp.zeros((tb, B_W), F32)
    for j in range(CONV_K):
        y = y + win_ref[pl.ds(HALO - CONV_K // 2 + j, tb), :] * w_ref[j:j + 1, :]
    y = y * jax.nn.sigmoid(y)
    pos = i * tb + lax.broadcasted_iota(jnp.int32, (tb, 1), 0)
    ok = (pos >= HEAD_PAD) & (pos < valid_end)
    unit = jnp.where(part == 0, HEAD_DIM ** -0.5, 1.0)
    for h in range(B_HEADS):
        yh = y[:, h * HEAD_DIM:(h + 1) * HEAD_DIM]
        inv = lax.rsqrt(jnp.sum(yh * yh, axis=-1, keepdims=True) + EPS) * unit
        fac = jnp.where(part == 2, 1.0, inv)
        qkv_ref[:, h * HEAD_DIM:(h + 1) * HEAD_DIM] = jnp.where(ok, yh * fac, 0.0)

    @pl.when(part == 0)
    def _():
        x = gate_ref[...]
        lane = lax.broadcasted_iota(jnp.int32, x.shape, 1)
        z = x + dtb_ref[...]
        softplus = jnp.maximum(z, 0.0) + jnp.log1p(jnp.exp(-jnp.abs(z)))
        g = -jnp.exp(a_ref[...]) * softplus
        beta = jax.nn.sigmoid(x)
        out = jnp.where(lane < GATE_BETA, g, jnp.where(lane < 2 * GATE_BETA, beta, 0.0))
        gb_ref[...] = jnp.where(ok, out, 0.0)


def dn_prep(proj, grp, conv_w, a_log, dt_bias):
    tb = grp.row_tile(384)
    nb = grp.lp // tb
    base = grp.base // tb
    hb = tb // HALO
    last_halo = proj.shape[0] // HALO - 1
    qkv_col = COL_BQKV // B_W
    pad = jnp.zeros((1, 128 - 2 * B_HEADS), F32)
    a_row = jnp.concatenate([a_log.astype(F32).reshape(1, 2 * B_HEADS), pad], axis=-1)
    dtb_row = jnp.concatenate([dt_bias.astype(F32).reshape(1, 2 * B_HEADS), pad], axis=-1)
    return pl.pallas_call(
        functools.partial(_dn_prep_kernel, tb=tb, valid_end=grp.valid_end),
        grid=(grp.bsz, nb, 3),
        in_specs=[
            pl.BlockSpec((HALO, B_W), lambda b, i, p: (jnp.maximum((base + b * nb + i) * hb - 1, 0), qkv_col + p)),
            pl.BlockSpec((tb, B_W), lambda b, i, p: (base + b * nb + i, qkv_col + p)),
            pl.BlockSpec((HALO, B_W), lambda b, i, p: (jnp.minimum((base + b * nb + i + 1) * hb, last_halo), qkv_col + p)),
            pl.BlockSpec((tb, 128), lambda b, i, p: (base + b * nb + i, COL_GATE // 128)),
            pl.BlockSpec((CONV_K, B_W), lambda b, i, p: (0, p)),
            pl.BlockSpec((1, 128), lambda b, i, p: (0, 0)),
            pl.BlockSpec((1, 128), lambda b, i, p: (0, 0)),
        ],
        out_specs=[
            pl.BlockSpec((tb, B_W), lambda b, i, p: (b * nb + i, p)),
            pl.BlockSpec((tb, 128), lambda b, i, p: (b * nb + i, 0)),
        ],
        out_shape=[
            jax.ShapeDtypeStruct((grp.total, 3 * B_W), F32),
            jax.ShapeDtypeStruct((grp.total, 128), F32),
        ],
        scratch_shapes=[pltpu.VMEM((tb + 2 * HALO, B_W), F32)],
        compiler_params=_params(("parallel", "parallel", "arbitrary")),
        name="dn_prep",
    )(proj, proj, proj, proj, conv_w, a_row, dtb_row)


def _nt(a, b):
    return lax.dot_general(a, b, (((1,), (1,)), ((), ())), preferred_element_type=F32)


def _dn_scan_kernel(qf_ref, kf_ref, vf_ref, gf_ref, qb_ref, kb_ref, vb_ref, gb_ref, of_ref, ob_ref, s_ref):
    @pl.when(pl.program_id(1) == 0)
    def _():
        s_ref[...] = jnp.zeros_like(s_ref)

    ri = lax.broadcasted_iota(jnp.int32, (CHUNK, CHUNK), 0)
    ci = lax.broadcasted_iota(jnp.int32, (CHUNK, CHUNK), 1)
    eye = (ri == ci).astype(F32)
    eye128 = (lax.broadcasted_iota(jnp.int32, (128, 128), 0) == lax.broadcasted_iota(jnp.int32, (128, 128), 1)).astype(F32)
    dirs = ((qf_ref, kf_ref, vf_ref, gf_ref, of_ref), (qb_ref, kb_ref, vb_ref, gb_ref, ob_ref))
    n = 2 * B_HEADS
    incl, strict, gcc, gcr, g_tot, beta, refs = [], [], [], [], [], [], []
    for d, (q_ref, k_ref, v_ref, g_ref, o_ref) in enumerate(dirs):
        incl_d = (ri >= ci) if d == 0 else (ri <= ci)
        strict_d = (ri > ci) if d == 0 else (ri < ci)
        gates = g_ref[...]
        cum = jnp.dot(incl_d.astype(F32), gates, preferred_element_type=F32, precision=lax.Precision.HIGHEST)
        cum_t = lax.dot_general(eye128, cum, (((1,), (1,)), ((), ())), preferred_element_type=F32,
                                precision=lax.Precision.HIGHEST)
        last = CHUNK - 1 if d == 0 else 0
        for h in range(B_HEADS):
            c = GATE_G + d * B_HEADS + h
            sl = slice(h * HEAD_DIM, (h + 1) * HEAD_DIM)
            incl.append(incl_d)
            strict.append(strict_d)
            gcc.append(cum[:, c:c + 1])
            gcr.append(cum_t[c:c + 1, :])
            g_tot.append(cum_t[c:c + 1, last:last + 1])
            beta.append(gates[:, GATE_BETA + c:GATE_BETA + c + 1])
            refs.append((q_ref, k_ref, v_ref, o_ref, sl))

    decay = [jnp.exp(jnp.where(incl[i], gcc[i] - gcr[i], -jnp.inf)) for i in range(n)]
    k = [refs[i][1][:, refs[i][4]] for i in range(n)]
    kb = [k[i] * beta[i] for i in range(n)]
    k16 = [k[i].astype(BF16) for i in range(n)]
    q = [refs[i][0][:, refs[i][4]] for i in range(n)]
    kk_qk = [_nt(jnp.concatenate([kb[i], q[i]], axis=0).astype(BF16), k16[i]) for i in range(n)]
    x = [-jnp.where(strict[i], kk_qk[i][:CHUNK] * decay[i], 0.0) for i in range(n)]
    qk = [jnp.where(incl[i], kk_qk[i][CHUNK:] * decay[i], 0.0).astype(BF16) for i in range(n)]
    t = [x[i] + eye for i in range(n)]
    x16 = [x[i].astype(BF16) for i in range(n)]
    p = [jnp.dot(x16[i], x16[i], preferred_element_type=F32) for i in range(n)]
    for _ in range(4):
        p16 = [p[i].astype(BF16) for i in range(n)]
        tp = [jnp.dot(jnp.concatenate([t[i].astype(BF16), p16[i]], axis=0), p16[i], preferred_element_type=F32)
              for i in range(n)]
        t = [t[i] + tp[i][:CHUNK] for i in range(n)]
        p = [tp[i][CHUNK:] for i in range(n)]
    t = [t[i] + jnp.dot(t[i].astype(BF16), p[i].astype(BF16), preferred_element_type=F32) for i in range(n)]
    egc = [jnp.exp(gcc[i]) for i in range(n)]
    rhs = [jnp.concatenate([kb[i] * egc[i], refs[i][2][:, refs[i][4]] * beta[i]], axis=-1).astype(BF16) for i in range(n)]
    wu = [jnp.dot(t[i].astype(BF16), rhs[i], preferred_element_type=F32) for i in range(n)]
    s = [s_ref[i] for i in range(n)]
    lhs = [jnp.concatenate([wu[i][:, :HEAD_DIM], q[i] * egc[i]], axis=0).astype(BF16) for i in range(n)]
    both = [jnp.dot(lhs[i], s[i].astype(BF16), preferred_element_type=F32) for i in range(n)]
    v16 = [(wu[i][:, HEAD_DIM:] - both[i][:CHUNK]).astype(BF16) for i in range(n)]
    for i in range(n):
        o_ref, sl = refs[i][3], refs[i][4]
        o_ref[:, sl] = both[i][CHUNK:] + jnp.dot(qk[i], v16[i], preferred_element_type=F32)
    kd = [(k[i] * jnp.exp(g_tot[i] - gcc[i])).astype(BF16) for i in range(n)]
    for i in range(n):
        s_ref[i] = s[i] * jnp.exp(g_tot[i]) + lax.dot_general(
            kd[i], v16[i], (((0,), (0,)), ((), ())), preferred_element_type=F32)


def dn_scan(qkv, gb, grp):
    nc = grp.lp // BLK

    def fw(col):
        return lambda b, c: (b * nc + c, col)

    def bw(col):
        return lambda b, c: (b * nc + nc - 1 - c, col)

    blk = (BLK, B_W)
    return pl.pallas_call(
        _dn_scan_kernel,
        grid=(grp.bsz, nc),
        in_specs=[
            pl.BlockSpec(blk, fw(0)), pl.BlockSpec(blk, fw(1)), pl.BlockSpec(blk, fw(2)), pl.BlockSpec((BLK, 128), fw(0)),
            pl.BlockSpec(blk, bw(0)), pl.BlockSpec(blk, bw(1)), pl.BlockSpec(blk, bw(2)), pl.BlockSpec((BLK, 128), bw(0)),
        ],
        out_specs=[pl.BlockSpec(blk, fw(0)), pl.BlockSpec(blk, bw(0))],
        out_shape=[jax.ShapeDtypeStruct((grp.total, B_W), F32)] * 2,
        scratch_shapes=[pltpu.VMEM((2 * B_HEADS, HEAD_DIM, HEAD_DIM), F32)],
        compiler_params=_params(("parallel", "arbitrary")),
        name="dn_scan",
    )(qkv, qkv, qkv, gb, qkv, qkv, qkv, gb)


def _dn_post_kernel(of_ref, ob_ref, z_ref, gain_ref, mix_ref, o_ref):
    for h in range(B_HEADS):
        lo, hi = h * HEAD_DIM, (h + 1) * HEAD_DIM
        o = of_ref[:, lo:hi] + ob_ref[:, lo:hi]
        ms = jnp.mean(o * o, axis=-1, keepdims=True)
        z = z_ref[:, lo:hi]
        o_ref[:, lo:hi] = (o * lax.rsqrt(ms + EPS) * gain_ref[...] * (z * jax.nn.sigmoid(z))).astype(o_ref.dtype)


def dn_post(o_fw, o_bw, proj, mix, grp, out_gain):
    tb = grp.row_tile(384)
    nb = grp.lp // tb
    base = grp.base // tb
    return pl.pallas_call(
        _dn_post_kernel,
        grid=(grp.bsz * nb,),
        in_specs=[
            pl.BlockSpec((tb, B_W), lambda i: (i, 0)),
            pl.BlockSpec((tb, B_W), lambda i: (i, 0)),
            pl.BlockSpec((tb, B_W), lambda i: (base + i, COL_BZ // B_W)),
            pl.BlockSpec((1, HEAD_DIM), lambda i: (0, 0)),
            pl.BlockSpec(memory_space=pl.ANY),
        ],
        out_specs=pl.BlockSpec((tb, B_W), lambda i: (base + i, MIX_COL_B // B_W)),
        out_shape=jax.ShapeDtypeStruct(mix.shape, mix.dtype),
        input_output_aliases={4: 0},
        compiler_params=_params(("parallel",)),
        name="dn_post",
    )(o_fw, o_bw, proj, out_gain.reshape(1, HEAD_DIM), mix)


def _permute_w_in(w_in):
    sizes = (A_Q_W, A_KV_W, A_KV_W, 3 * B_W, B_W, 2 * B_HEADS, 2 * B_HEADS, C_W, C_W, C_W)
    offs = [0]
    for s in sizes:
        offs.append(offs[-1] + s)
    w_in = w_in.astype(BF16)
    seg = [w_in[..., offs[n]:offs[n + 1]] for n in range(len(sizes))]
    aq, ak, av, bqkv, bz, ba, bb, cq, ck, cv = seg
    used = COL_GATE + 4 * B_HEADS
    pad = jnp.zeros(w_in.shape[:-1] + (IN_WIDTH_PAD - used,), w_in.dtype)
    return jnp.concatenate([aq, ak, av, bqkv, bz, cq, ck, cv, ba, bb, pad], axis=-1)


def _stack_tokens(xs, groups, meta_tokens):
    parts = []
    for x, grp in zip(xs, groups):
        d = x.shape[-1]
        front_meta = jnp.concatenate([jnp.zeros((HEAD_PAD, d), x.dtype), meta_tokens.astype(x.dtype)], axis=0)
        tail = jnp.zeros((grp.lp - grp.valid_end, d), x.dtype)
        for b in range(grp.bsz):
            parts += [front_meta, x[b], tail]
    return jnp.concatenate(parts, axis=0)


def kernel(x_prompt, x_sample, meta_tokens, ffn1_norm, ffn1_w_gate, ffn1_w_up, ffn1_w_down, mix_norm, w_in,
           attn_q_norm, attn_k_norm, dn_conv_w, dn_a_log, dn_dt_bias, dn_out_norm, na_rel_bias, na_meta_bias,
           w_out, ffn2_norm, ffn2_w_gate, ffn2_w_up, ffn2_w_down, final_norm):
    depth = w_in.shape[0]
    g_prompt = Group(x_prompt.shape[0], x_prompt.shape[1], 0)
    g_sample = Group(x_sample.shape[0], x_sample.shape[1], g_prompt.total)
    groups = (g_prompt, g_sample)
    x = _stack_tokens((x_prompt, x_sample), groups, meta_tokens)
    tables = [_position_tables(grp) for grp in groups]

    w1g, w1u, w1d = (w.astype(BF16) for w in (ffn1_w_gate, ffn1_w_up, ffn1_w_down))
    w2g, w2u, w2d = (w.astype(BF16) for w in (ffn2_w_gate, ffn2_w_up, ffn2_w_down))
    w_in_p = _permute_w_in(w_in)
    w_out_b = jnp.concatenate([w_out[:, A_Q_W:A_Q_W + B_W], w_out[:, :A_Q_W], w_out[:, A_Q_W + B_W:]], axis=1).astype(BF16)
    mix = jnp.zeros((x.shape[0], MIX_WIDTH), BF16)

    for l in range(depth):
        x = residual_matmul(x, norm_gate_up(x, ffn1_norm[l], w1g, w1u, l), w1d, l, 0.5)

        proj = norm_matmul(x, mix_norm[l], w_in_p, l, IN_TILE, F32)
        bias_table = _na_bias_table(na_rel_bias[l], na_meta_bias[l])
        for grp, (cos, sin, key_bias) in zip(groups, tables):
            qa, ka, va, qc, kvc = prep_ac(proj, grp, cos, sin, attn_q_norm[l], attn_k_norm[l])
            mix = flash_attention(qa, pad_keys(ka, grp), values_with_ones(pad_keys(va, grp)), key_bias, mix, grp)
            mix = neighbourhood_attention(qc, kvc, bias_table, mix, grp)
            qkv, gb = dn_prep(proj, grp, dn_conv_w[l], dn_a_log[l], dn_dt_bias[l])
            o_fw, o_bw = dn_scan(qkv, gb, grp)
            mix = dn_post(o_fw, o_bw, proj, mix, grp, dn_out_norm[l])
        x = residual_matmul(x, mix, w_out_b, l, 1.0)

        x = residual_matmul(x, norm_gate_up(x, ffn2_norm[l], w2g, w2u, l), w2d, l, 0.5)

    return tuple(final_rmsnorm(x, final_norm, grp) for grp in groups)
```

```python
import functools

import jax
import jax.numpy as jnp
from jax import lax
from jax.experimental import pallas as pl
from jax.experimental.pallas import tpu as pltpu

F32 = jnp.float32
BF16 = jnp.bfloat16

HEAD_DIM = 128
N_META = 16
GRID_W = 64
EPS = 1e-6
A_HEADS = 4
A_KV_HEADS = 2
ROPE_THETA = 10000.0
B_HEADS = 8
CONV_K = 5
CHUNK = 64
C_HEADS = 4
WIN_ROWS = 8
WIN_COLS = 16

A_Q_W = A_HEADS * HEAD_DIM
A_KV_W = A_KV_HEADS * HEAD_DIM
B_W = B_HEADS * HEAD_DIM
C_W = C_HEADS * HEAD_DIM
MIX_WIDTH = A_Q_W + B_W + C_W

BLK = 64
HEAD_PAD = BLK - N_META
SEQ_ALIGN = 128
NEG = -1e30
LOG2E = 1.4426950408889634

COL_AQ = 0
COL_AK = COL_AQ + A_Q_W
COL_AV = COL_AK + A_KV_W
COL_BQKV = COL_AV + A_KV_W
COL_BZ = COL_BQKV + 3 * B_W
COL_CQ = COL_BZ + B_W
COL_CK = COL_CQ + C_W
COL_CV = COL_CK + C_W
COL_GATE = COL_CV + C_W
MIX_COL_B = 0
MIX_COL_A = MIX_COL_B + B_W
MIX_COL_C = MIX_COL_A + A_Q_W
IN_TILE = 768
IN_WIDTH_PAD = -(-(COL_GATE + 128) // IN_TILE) * IN_TILE

V7X_VMEM_BYTES = 64 * 1024 * 1024
VMEM_LIMIT = 52 * 1024 * 1024


def _params(sem):
    return pltpu.CompilerParams(dimension_semantics=sem, vmem_limit_bytes=VMEM_LIMIT)


def _pick_tile(n, target, mult):
    best = None
    for t in range(mult, min(n, target) + 1, mult):
        if n % t == 0:
            best = t
    assert best is not None, (n, target, mult)
    return best


def _rmsnorm_kernel(x_ref, g_ref, o_ref):
    x = x_ref[...]
    ms = jnp.mean(x * x, axis=-1, keepdims=True)
    o_ref[...] = (x * lax.rsqrt(ms + EPS) * g_ref[...]).astype(o_ref.dtype)


NORM_SUB_ROWS = 16
NORM_UNROLL = 4


def _norm_rows(x_ref, g_ref, h_ref):
    @pl.when(pl.program_id(1) == 0)
    def _():
        rows = x_ref.shape[0]
        sub = NORM_SUB_ROWS
        unroll = max(u for u in range(1, NORM_UNROLL + 1) if rows % (sub * u) == 0)

        def body(r, carry):
            for u in range(unroll):
                rs = pl.ds(pl.multiple_of((r * unroll + u) * sub, sub), sub)
                x = x_ref[rs, :]
                ms = jnp.mean(x * x, axis=-1, keepdims=True)
                h_ref[rs, :] = (x * lax.rsqrt(ms + EPS) * g_ref[...]).astype(h_ref.dtype)
            return carry

        lax.fori_loop(0, rows // (sub * unroll), body, 0)


def _norm_gate_up_kernel(x_ref, g_ref, wg_ref, wu_ref, o_ref, h_ref):
    _norm_rows(x_ref, g_ref, h_ref)
    h = h_ref[...]
    g = jnp.dot(h, wg_ref[...], preferred_element_type=F32)
    u = jnp.dot(h, wu_ref[...], preferred_element_type=F32)
    o_ref[...] = (g * jax.nn.sigmoid(g) * u).astype(o_ref.dtype)


def norm_gate_up(x, gain, wg, wu, layer):
    t, d = x.shape
    f = wg.shape[-1]
    tm = _pick_tile(t, 640, 16)
    tn = _pick_tile(f, 512, 128)
    return pl.pallas_call(
        _norm_gate_up_kernel,
        grid=(t // tm, f // tn),
        in_specs=[
            pl.BlockSpec((tm, d), lambda i, j: (i, 0)),
            pl.BlockSpec((1, d), lambda i, j: (0, 0)),
            pl.BlockSpec((None, d, tn), lambda i, j: (layer, 0, j)),
            pl.BlockSpec((None, d, tn), lambda i, j: (layer, 0, j)),
        ],
        out_specs=pl.BlockSpec((tm, tn), lambda i, j: (i, j)),
        out_shape=jax.ShapeDtypeStruct((t, f), BF16),
        scratch_shapes=[pltpu.VMEM((tm, d), BF16)],
        compiler_params=_params(("parallel", "arbitrary")),
        name="norm_gate_up",
    )(x, gain.reshape(1, d), wg, wu)


def _norm_matmul_kernel(x_ref, g_ref, w_ref, o_ref, h_ref):
    _norm_rows(x_ref, g_ref, h_ref)
    o_ref[...] = jnp.dot(h_ref[...], w_ref[...], preferred_element_type=F32).astype(o_ref.dtype)


def norm_matmul(x, gain, w, layer, tn_target, out_dtype):
    t, d = x.shape
    n = w.shape[-1]
    tm = _pick_tile(t, 640, 16)
    tn = _pick_tile(n, tn_target, 128)
    return pl.pallas_call(
        _norm_matmul_kernel,
        grid=(t // tm, n // tn),
        in_specs=[
            pl.BlockSpec((tm, d), lambda i, j: (i, 0)),
            pl.BlockSpec((1, d), lambda i, j: (0, 0)),
            pl.BlockSpec((None, d, tn), lambda i, j: (layer, 0, j)),
        ],
        out_specs=pl.BlockSpec((tm, tn), lambda i, j: (i, j)),
        out_shape=jax.ShapeDtypeStruct((t, n), out_dtype),
        scratch_shapes=[pltpu.VMEM((tm, d), BF16)],
        compiler_params=_params(("parallel", "arbitrary")),
        name="norm_matmul",
    )(x, gain.reshape(1, d), w)


def _residual_matmul_kernel(x_ref, a_ref, w_ref, o_ref, *, scale):
    o_ref[...] = x_ref[...] + scale * jnp.dot(a_ref[...], w_ref[...], preferred_element_type=F32)


def residual_matmul(x, a, w, layer, scale):
    t, d = x.shape
    k = a.shape[-1]
    tm = _pick_tile(t, 640, 16)
    tn = _pick_tile(d, 1024, 128)
    return pl.pallas_call(
        functools.partial(_residual_matmul_kernel, scale=scale),
        grid=(d // tn, t // tm),
        in_specs=[
            pl.BlockSpec((tm, tn), lambda j, i: (i, j)),
            pl.BlockSpec((tm, k), lambda j, i: (i, 0)),
            pl.BlockSpec((None, k, tn), lambda j, i: (layer, 0, j)),
        ],
        out_specs=pl.BlockSpec((tm, tn), lambda j, i: (i, j)),
        out_shape=jax.ShapeDtypeStruct((t, d), F32),
        compiler_params=_params(("parallel", "parallel")),
        name="residual_matmul",
    )(x, a, w)


def final_rmsnorm(x, gain, grp):
    d = x.shape[-1]
    nblk = grp.lp // BLK
    base = grp.base // BLK
    rows = grp.n_tok // BLK
    return pl.pallas_call(
        _rmsnorm_kernel,
        grid=(grp.bsz, rows),
        in_specs=[
            pl.BlockSpec((BLK, d), lambda b, i: (base + b * nblk + 1 + i, 0)),
            pl.BlockSpec((1, d), lambda b, i: (0, 0)),
        ],
        out_specs=pl.BlockSpec((None, BLK, d), lambda b, i: (b, i, 0)),
        out_shape=jax.ShapeDtypeStruct((grp.bsz, grp.n_tok, d), F32),
        compiler_params=_params(("parallel", "parallel")),
        name="final_norm",
    )(x, gain.reshape(1, d))


class Group:
    def __init__(self, bsz, n_tok, base):
        assert n_tok % GRID_W == 0 and n_tok // GRID_W >= WIN_ROWS
        self.bsz = bsz
        self.n_tok = n_tok
        self.lp = -(-(BLK + n_tok) // SEQ_ALIGN) * SEQ_ALIGN
        self.base = base
        self.rows = n_tok // GRID_W
        self.valid_end = BLK + n_tok

    @property
    def total(self):
        return self.bsz * self.lp

    def row_tile(self, target):
        best = SEQ_ALIGN
        for t in range(SEQ_ALIGN, target + 1, SEQ_ALIGN):
            if self.lp % t == 0 and self.base % t == 0:
                best = t
        return best


def _position_tables(grp):
    p = jnp.arange(grp.lp, dtype=jnp.int32)
    t = jnp.clip(p - BLK, 0, grp.n_tok - 1)
    is_tok = (p >= BLK) & (p < grp.valid_end)
    row = jnp.where(is_tok, t // GRID_W, 0).astype(F32)
    col = jnp.where(is_tok, t % GRID_W, 0).astype(F32)
    half = HEAD_DIM // 4
    freqs = ROPE_THETA ** (-jnp.arange(half, dtype=F32) / half)
    ang_r = row[:, None] * freqs[None, :]
    ang_c = col[:, None] * freqs[None, :]
    cos = jnp.concatenate([jnp.cos(ang_r), jnp.cos(ang_r), jnp.cos(ang_c), jnp.cos(ang_c)], axis=-1)
    sin = jnp.concatenate([-jnp.sin(ang_r), jnp.sin(ang_r), -jnp.sin(ang_c), jnp.sin(ang_c)], axis=-1)
    pk = jnp.arange(-(-grp.lp // FLASH_CK) * FLASH_CK, dtype=jnp.int32)
    key_bias = jnp.where((pk >= HEAD_PAD) & (pk < grp.valid_end), 0.0, NEG).astype(F32)
    return cos, sin, key_bias


def _swap_halves(x):
    lane = lax.broadcasted_iota(jnp.int32, x.shape, 1)
    return jnp.where(lane % 64 < 32, pltpu.roll(x, 96, 1), pltpu.roll(x, 32, 1))


def _prep_ac_kernel(a_ref, cq_ref, ck_ref, cv_ref, cos_ref, sin_ref, qg_ref, kg_ref,
                    qa_ref, ka_ref, va_ref, qc_ref, kvc_ref):
    cos = cos_ref[...]
    sin = sin_ref[...]
    scale = HEAD_DIM ** -0.5

    def norm_rope(x, gain):
        ms = jnp.mean(x * x, axis=-1, keepdims=True)
        y = x * lax.rsqrt(ms + EPS) * gain
        return y * cos + _swap_halves(y) * sin

    for h in range(A_HEADS):
        x = a_ref[:, COL_AQ + h * HEAD_DIM:COL_AQ + (h + 1) * HEAD_DIM]
        qa_ref[:, h * HEAD_DIM:(h + 1) * HEAD_DIM] = (norm_rope(x, qg_ref[...]) * (scale * LOG2E)).astype(BF16)
    for h in range(A_KV_HEADS):
        x = a_ref[:, COL_AK + h * HEAD_DIM:COL_AK + (h + 1) * HEAD_DIM]
        ka_ref[:, h * HEAD_DIM:(h + 1) * HEAD_DIM] = norm_rope(x, kg_ref[...]).astype(BF16)
    va_ref[...] = a_ref[:, COL_AV:COL_AV + A_KV_W].astype(BF16)
    qc_ref[...] = (cq_ref[...] * scale).astype(BF16)
    kvc_ref[:, :C_W] = ck_ref[...].astype(BF16)
    kvc_ref[:, C_W:] = cv_ref[...].astype(BF16)


def prep_ac(proj, grp, cos, sin, q_gain, k_gain):
    tr = grp.row_tile(384)
    nb = grp.lp // tr
    base = grp.base // tr
    a_w = A_Q_W + 2 * A_KV_W

    def rowmap(col):
        return lambda b, i: (base + b * nb + i, col)

    out_map = lambda b, i: (b * nb + i, 0)
    n = grp.total
    return pl.pallas_call(
        _prep_ac_kernel,
        grid=(grp.bsz, nb),
        in_specs=[
            pl.BlockSpec((tr, a_w), rowmap(0)),
            pl.BlockSpec((tr, C_W), rowmap(COL_CQ // C_W)),
            pl.BlockSpec((tr, C_W), rowmap(COL_CK // C_W)),
            pl.BlockSpec((tr, C_W), rowmap(COL_CV // C_W)),
            pl.BlockSpec((tr, HEAD_DIM), lambda b, i: (i, 0)),
            pl.BlockSpec((tr, HEAD_DIM), lambda b, i: (i, 0)),
            pl.BlockSpec((1, HEAD_DIM), lambda b, i: (0, 0)),
            pl.BlockSpec((1, HEAD_DIM), lambda b, i: (0, 0)),
        ],
        out_specs=[
            pl.BlockSpec((tr, A_Q_W), out_map),
            pl.BlockSpec((tr, A_KV_W), out_map),
            pl.BlockSpec((tr, A_KV_W), out_map),
            pl.BlockSpec((tr, C_W), out_map),
            pl.BlockSpec((tr, 2 * C_W), out_map),
        ],
        out_shape=[
            jax.ShapeDtypeStruct((n, A_Q_W), BF16),
            jax.ShapeDtypeStruct((n, A_KV_W), BF16),
            jax.ShapeDtypeStruct((n, A_KV_W), BF16),
            jax.ShapeDtypeStruct((n, C_W), BF16),
            jax.ShapeDtypeStruct((n, 2 * C_W), BF16),
        ],
        compiler_params=_params(("parallel", "parallel")),
        name="prep_ac",
    )(proj, proj, proj, proj, cos, sin, q_gain.reshape(1, HEAD_DIM), k_gain.reshape(1, HEAD_DIM))


FLASH_CK = 256
FLASH_UNROLL = 10


def _flash_kernel(q_ref, k_ref, v_ref, bias_ref, mix_ref, o_ref, q2_ref, s_ref, m_ref, l_ref, acc_ref, *, tq, nchunks):
    q2_ref[:tq] = q_ref[:, :HEAD_DIM]
    q2_ref[tq:] = q_ref[:, HEAD_DIM:]
    q2 = q2_ref[...]

    def scores(t, slot):
        start = pl.multiple_of(t * FLASH_CK, FLASH_CK)
        k = k_ref[pl.ds(start, FLASH_CK), :]
        s_ref[slot] = lax.dot_general(q2, k, (((1,), (1,)), ((), ())), preferred_element_type=F32)

    def step(t, slot, masked, prefetch):
        if prefetch:
            scores(t + 1, 1 - slot)

        def read_scores():
            s = s_ref[slot]
            return s + bias_ref[pl.ds(t, 1), :] if masked else s

        m = m_ref[...]
        m_new = jnp.maximum(m, jnp.max(read_scores(), axis=-1, keepdims=True))
        m_ref[...] = m_new
        p = jnp.exp2(read_scores() - jnp.concatenate([m_new, m_new], axis=-1))
        alpha = jnp.exp2(m - m_new)
        v = v_ref[pl.ds(pl.multiple_of(t * FLASH_CK, FLASH_CK), FLASH_CK), :]
        pv = jnp.dot(p.astype(BF16), v, preferred_element_type=F32)
        l_ref[...] = alpha * l_ref[...] + pv[:, HEAD_DIM:]
        acc_ref[...] = alpha * acc_ref[...] + pv[:, :HEAD_DIM]

    m_ref[...] = jnp.full(m_ref.shape, -jnp.inf, F32)
    l_ref[...] = jnp.zeros(l_ref.shape, F32)
    acc_ref[...] = jnp.zeros(acc_ref.shape, F32)
    scores(0, 0)
    step(0, 0, True, nchunks > 1)
    if nchunks > 1:
        t0 = 1
        for _ in range((nchunks - 2) % FLASH_UNROLL):
            step(t0, t0 % 2, False, True)
            t0 += 1

        def group(j, carry):
            for u in range(FLASH_UNROLL):
                step(t0 + FLASH_UNROLL * j + u, (t0 + u) % 2, False, True)
            return carry

        lax.fori_loop(0, (nchunks - 1 - t0) // FLASH_UNROLL, group, 0)
        step(nchunks - 1, (nchunks - 1) % 2, True, False)
    o = acc_ref[...] / l_ref[...]
    o_ref[:, :HEAD_DIM] = o[:tq].astype(o_ref.dtype)
    o_ref[:, HEAD_DIM:] = o[tq:].astype(o_ref.dtype)


def flash_attention(qa, ka, va, key_bias, mix, grp):
    lp = grp.lp
    lkv = ka.shape[0] // grp.bsz
    assert lkv % FLASH_CK == 0 and lkv - grp.valid_end <= FLASH_CK and HEAD_PAD <= FLASH_CK
    tq = _pick_tile(lp, 384, 128)
    nchunks = lkv // FLASH_CK
    nq = lp // tq
    gw = 2 * HEAD_DIM
    return pl.pallas_call(
        functools.partial(_flash_kernel, tq=tq, nchunks=nchunks),
        grid=(grp.bsz, A_KV_HEADS, nq),
        in_specs=[
            pl.BlockSpec((tq, gw), lambda b, g, i: (b * nq + i, g)),
            pl.BlockSpec((lkv, HEAD_DIM), lambda b, g, i: (b, g)),
            pl.BlockSpec((lkv, 2 * HEAD_DIM), lambda b, g, i: (b, g)),
            pl.BlockSpec((nchunks, FLASH_CK), lambda b, g, i: (0, 0)),
            pl.BlockSpec(memory_space=pl.ANY),
        ],
        out_specs=pl.BlockSpec((tq, gw), lambda b, g, i: (grp.base // tq + b * nq + i, MIX_COL_A // gw + g)),
        out_shape=jax.ShapeDtypeStruct(mix.shape, mix.dtype),
        input_output_aliases={4: 0},
        scratch_shapes=[
            pltpu.VMEM((2 * tq, HEAD_DIM), BF16),
            pltpu.VMEM((2, 2 * tq, FLASH_CK), F32),
            pltpu.VMEM((2 * tq, HEAD_DIM), F32),
            pltpu.VMEM((2 * tq, HEAD_DIM), F32),
            pltpu.VMEM((2 * tq, HEAD_DIM), F32),
        ],
        compiler_params=_params(("parallel", "parallel", "parallel")),
        name="flash_attention",
    )(qa, ka, va, key_bias.reshape(nchunks, FLASH_CK), mix)


def values_with_ones(va):
    v = va.reshape(va.shape[0], A_KV_HEADS, HEAD_DIM)
    return jnp.concatenate([v, jnp.ones_like(v)], axis=-1).reshape(va.shape[0], 2 * A_KV_W)


def pad_keys(x, grp, value=0):
    lkv = -(-grp.lp // FLASH_CK) * FLASH_CK
    x = x.reshape(grp.bsz, grp.lp, -1)
    x = jnp.pad(x, ((0, 0), (0, lkv - grp.lp), (0, 0)), constant_values=value)
    return x.reshape(grp.bsz * lkv, -1)


NA_KEYS = BLK + WIN_ROWS * GRID_W
NA_META_VARIANT = WIN_ROWS


def _na_bias_table(rel_bias, meta_bias):
    rel_bias = rel_bias.astype(F32)
    qc = jnp.arange(GRID_W, dtype=jnp.int32)
    c = jnp.arange(GRID_W, dtype=jnp.int32)
    col_start = jnp.clip(qc - WIN_COLS // 2, 0, GRID_W - WIN_COLS)
    in_win = (c[None, :] >= col_start[:, None]) & (c[None, :] < col_start[:, None] + WIN_COLS)
    dc = jnp.clip(c[None, :] - qc[:, None] + (WIN_COLS - 1), 0, 2 * WIN_COLS - 2)
    full = jnp.where(in_win[None, None], rel_bias[:, :, dc], NEG)
    r = jnp.arange(WIN_ROWS, dtype=jnp.int32)
    variants = []
    for d0 in range(WIN_ROWS):
        band = full[:, d0 + r]
        variants.append(jnp.moveaxis(band, 1, 2).reshape(C_HEADS, GRID_W, WIN_ROWS * GRID_W))
    meta_band = jnp.broadcast_to(variants[WIN_ROWS - 1][:, :1], (C_HEADS, GRID_W, WIN_ROWS * GRID_W))
    variants.append(meta_band)
    band = jnp.stack(variants)
    mcol = jnp.concatenate([jnp.full((C_HEADS, HEAD_PAD), NEG, F32), meta_bias.astype(F32)], axis=-1)
    mcol = jnp.broadcast_to(mcol[None, :, None, :], (WIN_ROWS + 1, C_HEADS, GRID_W, BLK))
    return jnp.concatenate([mcol, band], axis=-1)


def _na_row_start(j, rows):
    return jnp.clip(j - 1 - WIN_ROWS // 2, 0, rows - WIN_ROWS)


NA_PAIR = 2


def _na_kernel(q_ref, meta_ref, *rest, rows):
    band_refs = rest[:NA_PAIR * WIN_ROWS]
    bias_ref, _, o_ref = rest[NA_PAIR * WIN_ROWS:]
    meta = meta_ref[...]
    items = []
    for u in range(NA_PAIR):
        j = jnp.minimum(pl.program_id(1) * NA_PAIR + u, rows)
        variant = jnp.where(j == 0, NA_META_VARIANT, _na_row_start(j, rows) - (j - 1) + (WIN_ROWS - 1))
        kv = jnp.concatenate([meta] + [r[...] for r in band_refs[u * WIN_ROWS:(u + 1) * WIN_ROWS]], axis=0)
        for h in range(C_HEADS):
            items.append((slice(u * BLK, (u + 1) * BLK), slice(h * HEAD_DIM, (h + 1) * HEAD_DIM), kv, variant, h))
    s = [lax.dot_general(q_ref[rs, sl], kv[:, sl], (((1,), (1,)), ((), ())), preferred_element_type=F32)
         + bias_ref[variant, h] for rs, sl, kv, variant, h in items]
    e = [jnp.exp(x - jnp.max(x, axis=-1, keepdims=True)) for x in s]
    p = [(x / jnp.sum(x, axis=-1, keepdims=True)).astype(BF16) for x in e]
    o = [jnp.dot(p[n], kv[:, C_W + sl.start:C_W + sl.stop], preferred_element_type=F32)
         for n, (rs, sl, kv, variant, h) in enumerate(items)]
    for n, (rs, sl, kv, variant, h) in enumerate(items):
        o_ref[rs, sl] = o[n].astype(o_ref.dtype)


def neighbourhood_attention(qc, kvc, bias_table, mix, grp):
    nblk = grp.lp // BLK
    rows = grp.rows
    assert nblk % NA_PAIR == 0
    npair = nblk // NA_PAIR
    qrows = NA_PAIR * BLK

    def band_map(u, r):
        def index_map(b, j):
            jj = jnp.minimum(j * NA_PAIR + u, rows)
            return (b * nblk + 1 + _na_row_start(jj, rows) + r, 0)
        return index_map

    in_specs = [
        pl.BlockSpec((qrows, C_W), lambda b, j: (b * npair + j, 0)),
        pl.BlockSpec((BLK, 2 * C_W), lambda b, j: (b * nblk, 0)),
    ]
    in_specs += [pl.BlockSpec((BLK, 2 * C_W), band_map(u, r)) for u in range(NA_PAIR) for r in range(WIN_ROWS)]
    in_specs += [pl.BlockSpec(bias_table.shape, lambda b, j: (0, 0, 0, 0)), pl.BlockSpec(memory_space=pl.ANY)]
    return pl.pallas_call(
        functools.partial(_na_kernel, rows=rows),
        grid=(grp.bsz, npair),
        in_specs=in_specs,
        out_specs=pl.BlockSpec((qrows, C_W), lambda b, j: (grp.base // qrows + b * npair + j, MIX_COL_C // C_W)),
        out_shape=jax.ShapeDtypeStruct(mix.shape, mix.dtype),
        input_output_aliases={len(in_specs) - 1: 0},
        compiler_params=_params(("parallel", "parallel")),
        name="neighbourhood_attention",
    )(qc, kvc, *([kvc] * (NA_PAIR * WIN_ROWS)), bias_table, mix)


HALO = 8
GATE_G = 0
GATE_BETA = 2 * B_HEADS


def _dn_prep_kernel(xp_ref, xc_ref, xn_ref, gate_ref, w_ref, a_ref, dtb_ref, qkv_ref, gb_ref, win_ref,
                    *, tb, valid_end):
    i = pl.program_id(1)
    part = pl.program_id(2)
    pos_w = i * tb - HALO + lax.broadcasted_iota(jnp.int32, (tb + 2 * HALO, 1), 0)
    ok_w = (pos_w >= HEAD_PAD) & (pos_w < valid_end)
    win_ref[0:HALO] = xp_ref[...]
    win_ref[HALO:HALO + tb] = xc_ref[...]
    win_ref[HALO + tb:] = xn_ref[...]
    win_ref[...] = jnp.where(ok_w, win_ref[...], 0.0)
    y = jnp.zeros((tb, B_W), F32)
    for j in range(CONV_K):
        y = y + win_ref[pl.ds(HALO - CONV_K // 2 + j, tb), :] * w_ref[j:j + 1, :]
    y = y * jax.nn.sigmoid(y)
    pos = i * tb + lax.broadcasted_iota(jnp.int32, (tb, 1), 0)
    ok = (pos >= HEAD_PAD) & (pos < valid_end)
    unit = jnp.where(part == 0, HEAD_DIM ** -0.5, 1.0)
    for h in range(B_HEADS):
        yh = y[:, h * HEAD_DIM:(h + 1) * HEAD_DIM]
        inv = lax.rsqrt(jnp.sum(yh * yh, axis=-1, keepdims=True) + EPS) * unit
        fac = jnp.where(part == 2, 1.0, inv)
        qkv_ref[:, h * HEAD_DIM:(h + 1) * HEAD_DIM] = jnp.where(ok, yh * fac, 0.0)

    @pl.when(part == 0)
    def _():
        x = gate_ref[...]
        lane = lax.broadcasted_iota(jnp.int32, x.shape, 1)
        z = x + dtb_ref[...]
        softplus = jnp.maximum(z, 0.0) + jnp.log1p(jnp.exp(-jnp.abs(z)))
        g = -jnp.exp(a_ref[...]) * softplus
        beta = jax.nn.sigmoid(x)
        out = jnp.where(lane < GATE_BETA, g, jnp.where(lane < 2 * GATE_BETA, beta, 0.0))
        gb_ref[...] = jnp.where(ok, out, 0.0)


def dn_prep(proj, grp, conv_w, a_log, dt_bias):
    tb = grp.row_tile(384)
    nb = grp.lp // tb
    base = grp.base // tb
    hb = tb // HALO
    last_halo = proj.shape[0] // HALO - 1
    qkv_col = COL_BQKV // B_W
    pad = jnp.zeros((1, 128 - 2 * B_HEADS), F32)
    a_row = jnp.concatenate([a_log.astype(F32).reshape(1, 2 * B_HEADS), pad], axis=-1)
    dtb_row = jnp.concatenate([dt_bias.astype(F32).reshape(1, 2 * B_HEADS), pad], axis=-1)
    return pl.pallas_call(
        functools.partial(_dn_prep_kernel, tb=tb, valid_end=grp.valid_end),
        grid=(grp.bsz, nb, 3),
        in_specs=[
            pl.BlockSpec((HALO, B_W), lambda b, i, p: (jnp.maximum((base + b * nb + i) * hb - 1, 0), qkv_col + p)),
            pl.BlockSpec((tb, B_W), lambda b, i, p: (base + b * nb + i, qkv_col + p)),
            pl.BlockSpec((HALO, B_W), lambda b, i, p: (jnp.minimum((base + b * nb + i + 1) * hb, last_halo), qkv_col + p)),
            pl.BlockSpec((tb, 128), lambda b, i, p: (base + b * nb + i, COL_GATE // 128)),
            pl.BlockSpec((CONV_K, B_W), lambda b, i, p: (0, p)),
            pl.BlockSpec((1, 128), lambda b, i, p: (0, 0)),
            pl.BlockSpec((1, 128), lambda b, i, p: (0, 0)),
        ],
        out_specs=[
            pl.BlockSpec((tb, B_W), lambda b, i, p: (b * nb + i, p)),
            pl.BlockSpec((tb, 128), lambda b, i, p: (b * nb + i, 0)),
        ],
        out_shape=[
            jax.ShapeDtypeStruct((grp.total, 3 * B_W), F32),
            jax.ShapeDtypeStruct((grp.total, 128), F32),
        ],
        scratch_shapes=[pltpu.VMEM((tb + 2 * HALO, B_W), F32)],
        compiler_params=_params(("parallel", "parallel", "arbitrary")),
        name="dn_prep",
    )(proj, proj, proj, proj, conv_w, a_row, dtb_row)


def _nt(a, b):
    return lax.dot_general(a, b, (((1,), (1,)), ((), ())), preferred_element_type=F32)


def _dn_scan_kernel(qf_ref, kf_ref, vf_ref, gf_ref, qb_ref, kb_ref, vb_ref, gb_ref, of_ref, ob_ref, s_ref):
    @pl.when(pl.program_id(1) == 0)
    def _():
        s_ref[...] = jnp.zeros_like(s_ref)

    ri = lax.broadcasted_iota(jnp.int32, (CHUNK, CHUNK), 0)
    ci = lax.broadcasted_iota(jnp.int32, (CHUNK, CHUNK), 1)
    eye = (ri == ci).astype(F32)
    eye128 = (lax.broadcasted_iota(jnp.int32, (128, 128), 0) == lax.broadcasted_iota(jnp.int32, (128, 128), 1)).astype(F32)
    dirs = ((qf_ref, kf_ref, vf_ref, gf_ref, of_ref), (qb_ref, kb_ref, vb_ref, gb_ref, ob_ref))
    n = 2 * B_HEADS
    incl, strict, gcc, gcr, g_tot, beta, refs = [], [], [], [], [], [], []
    for d, (q_ref, k_ref, v_ref, g_ref, o_ref) in enumerate(dirs):
        incl_d = (ri >= ci) if d == 0 else (ri <= ci)
        strict_d = (ri > ci) if d == 0 else (ri < ci)
        gates = g_ref[...]
        cum = jnp.dot(incl_d.astype(F32), gates, preferred_element_type=F32, precision=lax.Precision.HIGHEST)
        cum_t = lax.dot_general(eye128, cum, (((1,), (1,)), ((), ())), preferred_element_type=F32,
                                precision=lax.Precision.HIGHEST)
        last = CHUNK - 1 if d == 0 else 0
        for h in range(B_HEADS):
            c = GATE_G + d * B_HEADS + h
            sl = slice(h * HEAD_DIM, (h + 1) * HEAD_DIM)
            incl.append(incl_d)
            strict.append(strict_d)
            gcc.append(cum[:, c:c + 1])
            gcr.append(cum_t[c:c + 1, :])
            g_tot.append(cum_t[c:c + 1, last:last + 1])
            beta.append(gates[:, GATE_BETA + c:GATE_BETA + c + 1])
            refs.append((q_ref, k_ref, v_ref, o_ref, sl))

    decay = [jnp.exp(jnp.where(incl[i], gcc[i] - gcr[i], -jnp.inf)) for i in range(n)]
    k = [refs[i][1][:, refs[i][4]] for i in range(n)]
    kb = [k[i] * beta[i] for i in range(n)]
    k16 = [k[i].astype(BF16) for i in range(n)]
    q = [refs[i][0][:, refs[i][4]] for i in range(n)]
    kk_qk = [_nt(jnp.concatenate([kb[i], q[i]], axis=0).astype(BF16), k16[i]) for i in range(n)]
    x = [-jnp.where(strict[i], kk_qk[i][:CHUNK] * decay[i], 0.0) for i in range(n)]
    qk = [jnp.where(incl[i], kk_qk[i][CHUNK:] * decay[i], 0.0).astype(BF16) for i in range(n)]
    t = [x[i] + eye for i in range(n)]
    x16 = [x[i].astype(BF16) for i in range(n)]
    p = [jnp.dot(x16[i], x16[i], preferred_element_type=F32) for i in range(n)]
    for _ in range(4):
        p16 = [p[i].astype(BF16) for i in range(n)]
        tp = [jnp.dot(jnp.concatenate([t[i].astype(BF16), p16[i]], axis=0), p16[i], preferred_element_type=F32)
              for i in range(n)]
        t = [t[i] + tp[i][:CHUNK] for i in range(n)]
        p = [tp[i][CHUNK:] for i in range(n)]
    t = [t[i] + jnp.dot(t[i].astype(BF16), p[i].astype(BF16), preferred_element_type=F32) for i in range(n)]
    egc = [jnp.exp(gcc[i]) for i in range(n)]
    rhs = [jnp.concatenate([kb[i] * egc[i], refs[i][2][:, refs[i][4]] * beta[i]], axis=-1).astype(BF16) for i in range(n)]
    wu = [jnp.dot(t[i].astype(BF16), rhs[i], preferred_element_type=F32) for i in range(n)]
    s = [s_ref[i] for i in range(n)]
    lhs = [jnp.concatenate([wu[i][:, :HEAD_DIM], q[i] * egc[i]], axis=0).astype(BF16) for i in range(n)]
    both = [jnp.dot(lhs[i], s[i].astype(BF16), preferred_element_type=F32) for i in range(n)]
    v16 = [(wu[i][:, HEAD_DIM:] - both[i][:CHUNK]).astype(BF16) for i in range(n)]
    for i in range(n):
        o_ref, sl = refs[i][3], refs[i][4]
        o_ref[:, sl] = both[i][CHUNK:] + jnp.dot(qk[i], v16[i], preferred_element_type=F32)
    kd = [(k[i] * jnp.exp(g_tot[i] - gcc[i])).astype(BF16) for i in range(n)]
    for i in range(n):
        s_ref[i] = s[i] * jnp.exp(g_tot[i]) + lax.dot_general(
            kd[i], v16[i], (((0,), (0,)), ((), ())), preferred_element_type=F32)


def dn_scan(qkv, gb, grp):
    nc = grp.lp // BLK

    def fw(col):
        return lambda b, c: (b * nc + c, col)

    def bw(col):
        return lambda b, c: (b * nc + nc - 1 - c, col)

    blk = (BLK, B_W)
    return pl.pallas_call(
        _dn_scan_kernel,
        grid=(grp.bsz, nc),
        in_specs=[
            pl.BlockSpec(blk, fw(0)), pl.BlockSpec(blk, fw(1)), pl.BlockSpec(blk, fw(2)), pl.BlockSpec((BLK, 128), fw(0)),
            pl.BlockSpec(blk, bw(0)), pl.BlockSpec(blk, bw(1)), pl.BlockSpec(blk, bw(2)), pl.BlockSpec((BLK, 128), bw(0)),
        ],
        out_specs=[pl.BlockSpec(blk, fw(0)), pl.BlockSpec(blk, bw(0))],
        out_shape=[jax.ShapeDtypeStruct((grp.total, B_W), F32)] * 2,
        scratch_shapes=[pltpu.VMEM((2 * B_HEADS, HEAD_DIM, HEAD_DIM), F32)],
        compiler_params=_params(("parallel", "arbitrary")),
        name="dn_scan",
    )(qkv, qkv, qkv, gb, qkv, qkv, qkv, gb)


def _dn_post_kernel(of_ref, ob_ref, z_ref, gain_ref, mix_ref, o_ref):
    for h in range(B_HEADS):
        lo, hi = h * HEAD_DIM, (h + 1) * HEAD_DIM
        o = of_ref[:, lo:hi] + ob_ref[:, lo:hi]
        ms = jnp.mean(o * o, axis=-1, keepdims=True)
        z = z_ref[:, lo:hi]
        o_ref[:, lo:hi] = (o * lax.rsqrt(ms + EPS) * gain_ref[...] * (z * jax.nn.sigmoid(z))).astype(o_ref.dtype)


def dn_post(o_fw, o_bw, proj, mix, grp, out_gain):
    tb = grp.row_tile(384)
    nb = grp.lp // tb
    base = grp.base // tb
    return pl.pallas_call(
        _dn_post_kernel,
        grid=(grp.bsz * nb,),
        in_specs=[
            pl.BlockSpec((tb, B_W), lambda i: (i, 0)),
            pl.BlockSpec((tb, B_W), lambda i: (i, 0)),
            pl.BlockSpec((tb, B_W), lambda i: (base + i, COL_BZ // B_W)),
            pl.BlockSpec((1, HEAD_DIM), lambda i: (0, 0)),
            pl.BlockSpec(memory_space=pl.ANY),
        ],
        out_specs=pl.BlockSpec((tb, B_W), lambda i: (base + i, MIX_COL_B // B_W)),
        out_shape=jax.ShapeDtypeStruct(mix.shape, mix.dtype),
        input_output_aliases={4: 0},
        compiler_params=_params(("parallel",)),
        name="dn_post",
    )(o_fw, o_bw, proj, out_gain.reshape(1, HEAD_DIM), mix)


def _permute_w_in(w_in):
    sizes = (A_Q_W, A_KV_W, A_KV_W, 3 * B_W, B_W, 2 * B_HEADS, 2 * B_HEADS, C_W, C_W, C_W)
    offs = [0]
    for s in sizes:
        offs.append(offs[-1] + s)
    w_in = w_in.astype(BF16)
    seg = [w_in[..., offs[n]:offs[n + 1]] for n in range(len(sizes))]
    aq, ak, av, bqkv, bz, ba, bb, cq, ck, cv = seg
    used = COL_GATE + 4 * B_HEADS
    pad = jnp.zeros(w_in.shape[:-1] + (IN_WIDTH_PAD - used,), w_in.dtype)
    return jnp.concatenate([aq, ak, av, bqkv, bz, cq, ck, cv, ba, bb, pad], axis=-1)


def _stack_tokens(xs, groups, meta_tokens):
    parts = []
    for x, grp in zip(xs, groups):
        d = x.shape[-1]
        front_meta = jnp.concatenate([jnp.zeros((HEAD_PAD, d), x.dtype), meta_tokens.astype(x.dtype)], axis=0)
        tail = jnp.zeros((grp.lp - grp.valid_end, d), x.dtype)
        for b in range(grp.bsz):
            parts += [front_meta, x[b], tail]
    return jnp.concatenate(parts, axis=0)


def kernel(x_prompt, x_sample, meta_tokens, ffn1_norm, ffn1_w_gate, ffn1_w_up, ffn1_w_down, mix_norm, w_in,
           attn_q_norm, attn_k_norm, dn_conv_w, dn_a_log, dn_dt_bias, dn_out_norm, na_rel_bias, na_meta_bias,
           w_out, ffn2_norm, ffn2_w_gate, ffn2_w_up, ffn2_w_down, final_norm):
    depth = w_in.shape[0]
    g_prompt = Group(x_prompt.shape[0], x_prompt.shape[1], 0)
    g_sample = Group(x_sample.shape[0], x_sample.shape[1], g_prompt.total)
    groups = (g_prompt, g_sample)
    x = _stack_tokens((x_prompt, x_sample), groups, meta_tokens)
    tables = [_position_tables(grp) for grp in groups]

    w1g, w1u, w1d = (w.astype(BF16) for w in (ffn1_w_gate, ffn1_w_up, ffn1_w_down))
    w2g, w2u, w2d = (w.astype(BF16) for w in (ffn2_w_gate, ffn2_w_up, ffn2_w_down))
    w_in_p = _permute_w_in(w_in)
    w_out_b = jnp.concatenate([w_out[:, A_Q_W:A_Q_W + B_W], w_out[:, :A_Q_W], w_out[:, A_Q_W + B_W:]], axis=1).astype(BF16)
    mix = jnp.zeros((x.shape[0], MIX_WIDTH), BF16)

    for l in range(depth):
        x = residual_matmul(x, norm_gate_up(x, ffn1_norm[l], w1g, w1u, l), w1d, l, 0.5)

        proj = norm_matmul(x, mix_norm[l], w_in_p, l, IN_TILE, F32)
        bias_table = _na_bias_table(na_rel_bias[l], na_meta_bias[l])
        for grp, (cos, sin, key_bias) in zip(groups, tables):
            qa, ka, va, qc, kvc = prep_ac(proj, grp, cos, sin, attn_q_norm[l], attn_k_norm[l])
            mix = flash_attention(qa, pad_keys(ka, grp), values_with_ones(pad_keys(va, grp)), key_bias, mix, grp)
            mix = neighbourhood_attention(qc, kvc, bias_table, mix, grp)
            qkv, gb = dn_prep(proj, grp, dn_conv_w[l], dn_a_log[l], dn_dt_bias[l])
            o_fw, o_bw = dn_scan(qkv, gb, grp)
            mix = dn_post(o_fw, o_bw, proj, mix, grp, dn_out_norm[l])
        x = residual_matmul(x, mix, w_out_b, l, 1.0)

        x = residual_matmul(x, norm_gate_up(x, ffn2_norm[l], w2g, w2u, l), w2d, l, 0.5)

    return tuple(final_rmsnorm(x, final_norm, grp) for grp in groups)
```
